```python
import math
import jax
import jax.numpy as jnp
from jax import lax
import numpy as np


D_MODEL = 2048
BATCH = 1
SEQ = 8192
DEPTH = 2

GRID_W = 64
CTX_LEN = 256
EPS = 1e-6

DA_HEADS = 8
DA_QK_DIM = 64
DA_V_DIM = 2 * DA_QK_DIM
DA_WIDTH = DA_HEADS * DA_V_DIM
DA_SCALE = DA_QK_DIM ** -0.5
ROPE_BASE = 10000.0
Q_BLOCK = 128

HY_WIDTH = 1024
HY_ORDER = 2
HY_SHORT = 3
HY_EMB = 33
HY_BANDS = (HY_EMB - 1) // 2
HY_FFN = 64
HY_MIN_DECAY = math.log(1e-2) / 1.5
HY_MAX_DECAY = math.log(1e-2) / 0.3

CV_WIDTH = 1024
CV_KERNEL = 31

N_BRANCH = 3

N_EXPERTS = 32
TOP_K = 4
D_EXPERT = 1024
SWIGLU_LIMIT = 7.0
SWIGLU_ALPHA = 1.702
MOE_BLOCK = 128

Q_COLS = 2 * DA_HEADS * DA_QK_DIM
K_COLS = Q_COLS
V_COLS = DA_WIDTH
HY_COLS = (HY_ORDER + 1) * HY_WIDTH
CV_COLS = 2 * CV_WIDTH
GATE_COLS = N_BRANCH * D_MODEL
OFF_Q = 0
OFF_K = OFF_Q + Q_COLS
OFF_V = OFF_K + K_COLS
OFF_HY = OFF_V + V_COLS
OFF_CV = OFF_HY + HY_COLS
OFF_GATE = OFF_CV + CV_COLS
IN_COLS = OFF_GATE + GATE_COLS

kernel_name = 'hybrid_diffattn_hyena_conformer_moe_dit'


def rmsnorm(x, g):
    xf = x.astype(jnp.float32)
    y = xf * lax.rsqrt(jnp.mean(xf * xf, axis=-1, keepdims=True) + EPS)
    return (y * g.astype(jnp.float32)).astype(x.dtype)


def layernorm(x, g, b):
    xf = x.astype(jnp.float32)
    mu = jnp.mean(xf, axis=-1, keepdims=True)
    var = jnp.mean(jnp.square(xf - mu), axis=-1, keepdims=True)
    y = (xf - mu) * lax.rsqrt(var + EPS)
    return (y * g.astype(jnp.float32) + b.astype(jnp.float32)).astype(x.dtype)


def depthwise_conv(x, w, b):
    k = w.shape[0]
    y = lax.conv_general_dilated(x, w[:, None, :].astype(x.dtype), (1,), [((k - 1) // 2, k // 2)],
                                 dimension_numbers=('NWC', 'WIO', 'NWC'),
                                 feature_group_count=x.shape[-1])
    return y + b.astype(x.dtype)


def axial_rope_tables(n_lat):
    rows = n_lat // GRID_W
    row = jnp.repeat(jnp.arange(rows, dtype=jnp.float32), GRID_W)
    col = jnp.tile(jnp.arange(GRID_W, dtype=jnp.float32), rows)
    half = DA_QK_DIM // 2
    inv = ROPE_BASE ** (-jnp.arange(0, half, 2, dtype=jnp.float32) / half)
    ar = row[:, None] * inv
    ac = col[:, None] * inv
    return (jnp.cos(ar), jnp.sin(ar), jnp.cos(ac), jnp.sin(ac))


def rope_axis(x, cos, sin):
    x1, x2 = jnp.split(x, 2, axis=-1)
    cos = cos[None, :, None, :]
    sin = sin[None, :, None, :]
    return jnp.concatenate([x1 * cos - x2 * sin, x2 * cos + x1 * sin], axis=-1)


def apply_axial_rope(x, tabs):
    xf = x.astype(jnp.float32)
    half = DA_QK_DIM // 2
    xr = rope_axis(xf[..., :half], tabs[0], tabs[1])
    xc = rope_axis(xf[..., half:], tabs[2], tabs[3])
    return jnp.concatenate([xr, xc], axis=-1).astype(x.dtype)


def diff_attend(q, k, v, lam):
    s = jnp.einsum('bqhd,bkhd->bhqk', q, k, preferred_element_type=jnp.float32) * DA_SCALE
    pr = jax.nn.softmax(s, axis=-1)
    b, _, nq, nk = pr.shape
    pr = pr.reshape(b, DA_HEADS, 2, nq, nk)
    w = pr[:, :, 0] - lam * pr[:, :, 1]
    return jnp.einsum('bhqk,bkhd->bqhd', w.astype(v.dtype), v)


def diff_head_out(o, g, lam_init):
    o = rmsnorm(o, g) * (1.0 - lam_init)
    return o.reshape(o.shape[0], o.shape[1], DA_WIDTH)


def hyena_filters(n, p):
    f32 = jnp.float32
    t = jnp.linspace(0.0, 1.0, n, dtype=f32)[:, None]
    w = 2.0 * math.pi * jnp.arange(n, dtype=f32)[:, None] / n
    bands = jnp.linspace(1e-4, HY_BANDS - 1, HY_BANDS, dtype=f32)[None, :]
    z = jnp.concatenate([t, jnp.cos(bands * w), -jnp.sin(bands * w)], axis=-1)
    freq = p['hy_freq'].astype(f32)
    h = jnp.sin(freq * (z @ p['hy_w1'].astype(f32) + p['hy_b1'].astype(f32)))
    h = jnp.sin(freq * (h @ p['hy_w2'].astype(f32) + p['hy_b2'].astype(f32)))
    h = (h @ p['hy_w3'].astype(f32)).reshape(n, HY_ORDER, 2, HY_WIDTH)
    deltas = jnp.abs(jnp.linspace(HY_MIN_DECAY, HY_MAX_DECAY, HY_WIDTH, dtype=f32))
    h = h * jnp.exp(-t * deltas)[:, None, None, :]
    h = h / jnp.sum(jnp.abs(h), axis=(0, 2), keepdims=True)
    return jnp.concatenate([h[:, :, 0], h[::-1, :, 1]], axis=0)


def long_conv(z, kf, bias):
    n = z.shape[1]
    zf = jnp.fft.rfft(z.astype(jnp.float32), n=2 * n, axis=1)
    kff = jnp.fft.rfft(kf, n=2 * n, axis=0)
    y = jnp.fft.irfft(zf * kff[None], n=2 * n, axis=1)[:, :n]
    return (y + z.astype(jnp.float32) * bias.astype(jnp.float32)).astype(z.dtype)


def hyena_branch(u, p):
    u = depthwise_conv(u, p['hy_short_w'], p['hy_short_b'])
    x1, x2, v = jnp.split(u, 3, axis=-1)
    kf = hyena_filters(u.shape[1], p)
    z = x1 * long_conv(v, kf[:, 0], p['hy_bias'][0])
    z = x2 * long_conv(z, kf[:, 1], p['hy_bias'][1])
    return z


def conv_branch(u, p):
    a, g = jnp.split(u, 2, axis=-1)
    y = a * jax.nn.sigmoid(g)
    y = depthwise_conv(y, p['cv_dw_w'], p['cv_dw_b'])
    y = layernorm(y, p['cv_ln_g'], p['cv_ln_b'])
    return jax.nn.silu(y)


def merge_branches(proj, a, hy, cv, p):
    b, n, _ = proj.shape
    g = jax.nn.sigmoid(proj[..., OFF_GATE:] + p['b_gate']).reshape(b, n, N_BRANCH, D_MODEL)
    y = (g[:, :, 0] * (a @ p['w_da_out']) + g[:, :, 1] * (hy @ p['w_hy_out'])
         + g[:, :, 2] * (cv @ p['w_cv_out']))
    return y @ p['w_out']


def token_mixer(h_lat, h_ctx, p, layer_idx, need_ctx):
    b, s, _ = h_lat.shape
    lc = h_ctx.shape[1]
    lam_init = 0.8 - 0.6 * math.exp(-0.3 * layer_idx)
    lp = p['da_lambda'].astype(jnp.float32)
    lam = jnp.exp(jnp.sum(lp[0] * lp[1])) - jnp.exp(jnp.sum(lp[2] * lp[3])) + lam_init

    proj = h_lat @ p['w_in']
    tabs = axial_rope_tables(s)
    q = apply_axial_rope(proj[..., OFF_Q:OFF_K].reshape(b, s, 2 * DA_HEADS, DA_QK_DIM), tabs)
    k = apply_axial_rope(proj[..., OFF_K:OFF_V].reshape(b, s, 2 * DA_HEADS, DA_QK_DIM), tabs)
    v = proj[..., OFF_V:OFF_HY].reshape(b, s, DA_HEADS, DA_V_DIM)

    if need_ctx:
        proj_c = h_ctx @ p['w_in']
        kv_c = proj_c[..., OFF_K:OFF_HY]
    else:
        kv_c = h_ctx @ p['w_in'][:, OFF_K:OFF_HY]
    k_c = kv_c[..., :K_COLS].reshape(b, lc, 2 * DA_HEADS, DA_QK_DIM)
    v_c = kv_c[..., K_COLS:].reshape(b, lc, DA_HEADS, DA_V_DIM)

    k_all = jnp.concatenate([k_c, k], axis=1)
    v_all = jnp.concatenate([v_c, v], axis=1)
    n_blk = s // Q_BLOCK
    q_blocks = jnp.moveaxis(q.reshape(b, n_blk, Q_BLOCK, 2 * DA_HEADS, DA_QK_DIM), 1, 0)
    o = lax.map(lambda qb: diff_attend(qb, k_all, v_all, lam), q_blocks)
    o = jnp.moveaxis(o, 0, 1).reshape(b, s, DA_HEADS, DA_V_DIM)
    a_lat = diff_head_out(o, p['da_subln_g'], lam_init)
    y_lat = merge_branches(proj, a_lat, hyena_branch(proj[..., OFF_HY:OFF_CV], p),
                           conv_branch(proj[..., OFF_CV:OFF_GATE], p), p)
    if not need_ctx:
        return y_lat, None

    q_c = proj_c[..., OFF_Q:OFF_K].reshape(b, lc, 2 * DA_HEADS, DA_QK_DIM)
    a_ctx = diff_head_out(diff_attend(q_c, k_c, v_c, lam), p['da_subln_g'], lam_init)
    y_ctx = merge_branches(proj_c, a_ctx, hyena_branch(proj_c[..., OFF_HY:OFF_CV], p),
                           conv_branch(proj_c[..., OFF_CV:OFF_GATE], p), p)
    return y_lat, y_ctx


def moe(h, p):
    n, d = h.shape
    logits = (h @ p['w_router']).astype(jnp.float32) + p['b_router'].astype(jnp.float32)
    top_val, top_idx = lax.top_k(logits, TOP_K)
    gates = jax.nn.softmax(top_val, axis=-1)
    flat_e = top_idx.reshape(-1)
    order = jnp.argsort(flat_e)
    sorted_e = flat_e[order]
    tok = order // TOP_K
    w_sorted = gates.reshape(-1)[order]
    counts = jnp.bincount(flat_e, length=N_EXPERTS)
    padded = (counts + MOE_BLOCK - 1) // MOE_BLOCK * MOE_BLOCK
    pad_end = jnp.cumsum(padded)
    pad_start = pad_end - padded
    grp_start = jnp.cumsum(counts) - counts
    rank = jnp.arange(n * TOP_K) - grp_start[sorted_e]
    dest = pad_start[sorted_e] + rank
    n_blocks = -(-(n * TOP_K) // MOE_BLOCK) + N_EXPERTS
    cap = n_blocks * MOE_BLOCK
    row_tok = jnp.zeros((cap,), jnp.int32).at[dest].set(tok.astype(jnp.int32))
    row_w = jnp.zeros((cap,), jnp.float32).at[dest].set(w_sorted)
    block_e = jnp.minimum(jnp.searchsorted(pad_end, jnp.arange(n_blocks) * MOE_BLOCK, side='right'),
                          N_EXPERTS - 1)
    xs = h[row_tok].reshape(n_blocks, MOE_BLOCK, d)

    def expert_block(args):
        xb, e = args
        gu = xb @ p['w_gu'][e] + p['b_gu'][e]
        gate = jnp.minimum(gu[:, :D_EXPERT], SWIGLU_LIMIT)
        up = jnp.clip(gu[:, D_EXPERT:], -SWIGLU_LIMIT, SWIGLU_LIMIT)
        act = gate * jax.nn.sigmoid(SWIGLU_ALPHA * gate) * (up + 1.0)
        return act @ p['w_dn'][e] + p['b_dn'][e]

    ys = lax.map(expert_block, (xs, block_e)).reshape(cap, d)
    out = jnp.zeros((n, d), jnp.float32).at[row_tok].add(ys.astype(jnp.float32) * row_w[:, None])
    return out.astype(h.dtype)


def trunk_layer(x, ctx, c, c_ctx, p, layer_idx, need_ctx):
    b, s, d = x.shape
    mod = jax.nn.silu(c) @ p['w_ada'] + p['b_ada']
    mod_c = (jax.nn.silu(c_ctx) @ p['w_ada'] + p['b_ada'])[None]
    sh1, sc1, g1, sh2, sc2, g2 = [m[:, None, :] for m in jnp.split(mod, 6, axis=-1)]
    csh1, csc1, cg1, csh2, csc2, cg2 = [m[:, None, :] for m in jnp.split(mod_c, 6, axis=-1)]
    ng = p['norm_g']

    h_lat = rmsnorm(x, ng[0]) * (1.0 + sc1) + sh1
    h_ctx = rmsnorm(ctx, ng[0]) * (1.0 + csc1) + csh1
    m_lat, m_ctx = token_mixer(h_lat, h_ctx, p, layer_idx, need_ctx)
    x = x + g1 * rmsnorm(m_lat, ng[1])
    f_lat_in = rmsnorm(x, ng[2]) * (1.0 + sc2) + sh2
    if need_ctx:
        ctx = ctx + cg1 * rmsnorm(m_ctx, ng[1])
        f_ctx_in = rmsnorm(ctx, ng[2]) * (1.0 + csc2) + csh2
        tokens = jnp.concatenate([f_ctx_in.reshape(-1, d), f_lat_in.reshape(-1, d)], axis=0)
        f = moe(tokens, p)
        n_c = ctx.shape[0] * ctx.shape[1]
        f_ctx = f[:n_c].reshape(ctx.shape)
        f_lat = f[n_c:].reshape(x.shape)
        ctx = ctx + cg2 * rmsnorm(f_ctx, ng[3])
    else:
        f_lat = moe(f_lat_in.reshape(-1, d), p).reshape(x.shape)
    x = x + g2 * rmsnorm(f_lat, ng[3])
    return x, ctx


def setup_inputs(seed: int = 0) -> dict:
    key = jax.random.key(seed)
    ks = iter(jax.random.split(key, 40))
    f32 = jnp.float32

    def nrm(shape, scale):
        return scale * jax.random.normal(next(ks), shape, f32)

    d, nl = D_MODEL, DEPTH
    return {
        'x': nrm((BATCH, SEQ, d), 1.0),
        'c': nrm((BATCH, d), 1.0),
        'ctx': nrm((BATCH, CTX_LEN, d), 1.0),
        'c_ctx': nrm((d,), 1.0),
        'w_ada': nrm((nl, d, 6 * d), 0.5 * d ** -0.5),
        'b_ada': nrm((nl, 6 * d), 0.01),
        'norm_g': 1.0 + nrm((nl, 4, d), 0.01),
        'w_in': nrm((nl, d, IN_COLS), d ** -0.5),
        'b_gate': nrm((nl, GATE_COLS), 0.01),
        'da_lambda': nrm((nl, 4, DA_QK_DIM), 0.1),
        'da_subln_g': 1.0 + nrm((nl, DA_V_DIM), 0.01),
        'w_da_out': nrm((nl, DA_WIDTH, d), DA_WIDTH ** -0.5),
        'hy_short_w': nrm((nl, HY_SHORT, HY_COLS), HY_SHORT ** -0.5),
        'hy_short_b': nrm((nl, HY_COLS), 0.01),
        'hy_w1': nrm((nl, HY_EMB, HY_FFN), HY_EMB ** -0.5),
        'hy_b1': nrm((nl, HY_FFN), 0.01),
        'hy_w2': nrm((nl, HY_FFN, HY_FFN), HY_FFN ** -0.5),
        'hy_b2': nrm((nl, HY_FFN), 0.01),
        'hy_freq': 1.0 + nrm((nl, HY_FFN), 0.01),
        'hy_w3': nrm((nl, HY_FFN, HY_ORDER * 2 * HY_WIDTH), HY_FFN ** -0.5),
        'hy_bias': nrm((nl, HY_ORDER, HY_WIDTH), 0.5),
        'w_hy_out': nrm((nl, HY_WIDTH, d), HY_WIDTH ** -0.5),
        'cv_dw_w': nrm((nl, CV_KERNEL, CV_WIDTH), CV_KERNEL ** -0.5),
        'cv_dw_b': nrm((nl, CV_WIDTH), 0.01),
        'cv_ln_g': 1.0 + nrm((nl, CV_WIDTH), 0.01),
        'cv_ln_b': nrm((nl, CV_WIDTH), 0.01),
        'w_cv_out': nrm((nl, CV_WIDTH, d), CV_WIDTH ** -0.5),
        'w_out': nrm((nl, d, d), d ** -0.5),
        'w_router': nrm((nl, d, N_EXPERTS), d ** -0.5),
        'b_router': nrm((nl, N_EXPERTS), 0.01),
        'w_gu': nrm((nl, N_EXPERTS, d, 2 * D_EXPERT), d ** -0.5),
        'b_gu': nrm((nl, N_EXPERTS, 2 * D_EXPERT), 0.01),
        'w_dn': nrm((nl, N_EXPERTS, D_EXPERT, d), D_EXPERT ** -0.5),
        'b_dn': nrm((nl, N_EXPERTS, d), 0.01),
    }


def reference(x, c, ctx, c_ctx, w_ada, b_ada, norm_g, w_in, b_gate, da_lambda, da_subln_g, w_da_out,
              hy_short_w, hy_short_b, hy_w1, hy_b1, hy_w2, hy_b2, hy_freq, hy_w3, hy_bias, w_hy_out,
              cv_dw_w, cv_dw_b, cv_ln_g, cv_ln_b, w_cv_out, w_out, w_router, b_router,
              w_gu, b_gu, w_dn, b_dn):
    for l in range(DEPTH):
        p = {
            'w_ada': w_ada[l], 'b_ada': b_ada[l], 'norm_g': norm_g[l],
            'w_in': w_in[l], 'b_gate': b_gate[l],
            'da_lambda': da_lambda[l], 'da_subln_g': da_subln_g[l], 'w_da_out': w_da_out[l],
            'hy_short_w': hy_short_w[l], 'hy_short_b': hy_short_b[l],
            'hy_w1': hy_w1[l], 'hy_b1': hy_b1[l], 'hy_w2': hy_w2[l], 'hy_b2': hy_b2[l],
            'hy_freq': hy_freq[l], 'hy_w3': hy_w3[l], 'hy_bias': hy_bias[l], 'w_hy_out': w_hy_out[l],
            'cv_dw_w': cv_dw_w[l], 'cv_dw_b': cv_dw_b[l], 'cv_ln_g': cv_ln_g[l], 'cv_ln_b': cv_ln_b[l],
            'w_cv_out': w_cv_out[l], 'w_out': w_out[l],
            'w_router': w_router[l], 'b_router': b_router[l],
            'w_gu': w_gu[l], 'b_gu': b_gu[l], 'w_dn': w_dn[l], 'b_dn': b_dn[l],
        }
        x, ctx = trunk_layer(x, ctx, c, c_ctx, p, l, l < DEPTH - 1)
    return x
```

```python
import functools
import math

import jax
import jax.numpy as jnp
from jax import lax
from jax.experimental import pallas as pl
from jax.experimental.pallas import tpu as pltpu

F32 = jnp.float32
BF16 = jnp.bfloat16

D_MODEL = 2048
DEPTH = 2
GRID_W = 64
EPS = 1e-6

DA_HEADS = 8
DA_QK_DIM = 64
DA_V_DIM = 2 * DA_QK_DIM
DA_WIDTH = DA_HEADS * DA_V_DIM
DA_SCALE = DA_QK_DIM ** -0.5
ROPE_BASE = 10000.0

HY_WIDTH = 1024
HY_ORDER = 2
HY_EMB = 33
HY_BANDS = (HY_EMB - 1) // 2
HY_MIN_DECAY = math.log(1e-2) / 1.5
HY_MAX_DECAY = math.log(1e-2) / 0.3

CV_WIDTH = 1024
N_BRANCH = 3

N_EXPERTS = 32
TOP_K = 4
D_EXPERT = 1024
SWIGLU_LIMIT = 7.0
SWIGLU_ALPHA = 1.702

Q_COLS = 2 * DA_HEADS * DA_QK_DIM
K_COLS = Q_COLS
V_COLS = DA_WIDTH
HY_COLS = (HY_ORDER + 1) * HY_WIDTH
CV_COLS = 2 * CV_WIDTH
GATE_COLS = N_BRANCH * D_MODEL
OFF_Q = 0
OFF_K = OFF_Q + Q_COLS
OFF_V = OFF_K + K_COLS
OFF_HY = OFF_V + V_COLS
OFF_CV = OFF_HY + HY_COLS
OFF_GATE = OFF_CV + CV_COLS
IN_COLS = OFF_GATE + GATE_COLS

VMEM_LIMIT_BYTES = 56 * 1024 * 1024
MOE_ROWS = 256


def _params(*sem):
    return pltpu.CompilerParams(dimension_semantics=sem, vmem_limit_bytes=VMEM_LIMIT_BYTES)


def _mm_kernel(x_ref, w_ref, o_ref, wb_ref):
    @pl.when(pl.program_id(1) == 0)
    def _():
        wb_ref[...] = w_ref[...].astype(BF16)

    o_ref[...] = jnp.dot(x_ref[...].astype(BF16), wb_ref[...],
                         preferred_element_type=F32).astype(o_ref.dtype)


def _mm(x, w, layer, *, col_off=0, n_cols=None, out_dtype=F32, bm=1024, bn=1024):
    m, k = x.shape
    n = w.shape[2] if n_cols is None else n_cols
    bm = min(bm, m)
    bn = min(bn, n)
    assert m % bm == 0 and n % bn == 0 and col_off % bn == 0
    joff = col_off // bn
    return pl.pallas_call(
        _mm_kernel,
        grid=(n // bn, m // bm),
        in_specs=[pl.BlockSpec((bm, k), lambda j, i: (i, 0)),
                  pl.BlockSpec((None, k, bn), lambda j, i: (layer, 0, j + joff))],
        out_specs=pl.BlockSpec((bm, bn), lambda j, i: (i, j)),
        out_shape=jax.ShapeDtypeStruct((m, n), out_dtype),
        scratch_shapes=[pltpu.VMEM((k, bn), BF16)],
        compiler_params=_params("arbitrary", "arbitrary"),
        name="dense_mm",
    )(x, w)


def _attn_kernel(lp_ref, g_ref, q_ref, kt_ref, v_ref, o_ref, *, bk, lam_init):
    bq = q_ref.shape[0]
    nk = kt_ref.shape[1] // bk
    q = q_ref[...]
    lane = lax.broadcasted_iota(jnp.int32, q.shape, 1)
    zero = jnp.zeros_like(q)
    q2 = jnp.concatenate([jnp.where(lane < DA_QK_DIM, q, zero),
                          jnp.where(lane >= DA_QK_DIM, q, zero)], axis=0)

    def body(i, carry):
        m, l, acc = carry
        off = pl.multiple_of(i * bk, bk)
        s = jnp.dot(q2, kt_ref[:, pl.ds(off, bk)], preferred_element_type=F32)
        m_new = jnp.maximum(m, jnp.max(s, axis=-1, keepdims=True))
        alpha = jnp.exp(m - m_new)
        p = jnp.exp(s - m_new)
        l = alpha * l + jnp.sum(p, axis=-1, keepdims=True)
        acc = alpha * acc + jnp.dot(p.astype(BF16), v_ref[pl.ds(off, bk), :],
                                    preferred_element_type=F32)
        return m_new, l, acc

    m0 = jnp.full((2 * bq, 1), -jnp.inf, F32)
    l0 = jnp.zeros((2 * bq, 1), F32)
    a0 = jnp.zeros((2 * bq, DA_V_DIM), F32)
    _, l, acc = lax.fori_loop(0, nk, body, (m0, l0, a0))
    o = acc / l
    lp = lp_ref[...].astype(F32)
    lam = (jnp.exp(jnp.sum(lp[0:1] * lp[1:2], axis=-1, keepdims=True))
           - jnp.exp(jnp.sum(lp[2:3] * lp[3:4], axis=-1, keepdims=True)) + lam_init)
    o = o[:bq] - lam * o[bq:]
    y = o * lax.rsqrt(jnp.mean(o * o, axis=-1, keepdims=True) + EPS)
    y = y * g_ref[...].astype(F32) * (1.0 - lam_init)
    o_ref[...] = y.astype(o_ref.dtype)


def _diff_attention(q, kt, v, da_lambda, da_subln_g, layer, lam_init, *, bq=256, bk=None):
    sq = q.shape[0]
    sk = v.shape[0]
    bq = min(bq, sq)
    if bk is None:
        bk = 768 if sk % 768 == 0 else sk
    hw = 2 * DA_QK_DIM
    return pl.pallas_call(
        functools.partial(_attn_kernel, bk=bk, lam_init=lam_init),
        grid=(DA_HEADS, sq // bq),
        in_specs=[pl.BlockSpec((None, 4, DA_QK_DIM), lambda h, i: (layer, 0, 0)),
                  pl.BlockSpec((None, 1, DA_V_DIM), lambda h, i: (layer, 0, 0)),
                  pl.BlockSpec((bq, hw), lambda h, i: (i, h)),
                  pl.BlockSpec((hw, sk), lambda h, i: (h, 0)),
                  pl.BlockSpec((sk, DA_V_DIM), lambda h, i: (0, h))],
        out_specs=pl.BlockSpec((bq, DA_V_DIM), lambda h, i: (i, h)),
        out_shape=jax.ShapeDtypeStruct((sq, DA_WIDTH), BF16),
        compiler_params=_params("arbitrary", "arbitrary"),
        name="diff_attention",
    )(da_lambda, da_subln_g.reshape(DEPTH, 1, DA_V_DIM), q, kt, v)


def _moe_gu_kernel(be_ref, first_ref, x_ref, wg_ref, wu_ref, bg_ref, bu_ref, o_ref, wgb_ref, wub_ref):
    it = pl.program_id(1)

    @pl.when(first_ref[it] == 1)
    def _():
        wgb_ref[...] = wg_ref[...].astype(BF16)
        wub_ref[...] = wu_ref[...].astype(BF16)

    x = x_ref[...]
    gate = jnp.dot(x, wgb_ref[...], preferred_element_type=F32) + bg_ref[...]
    up = jnp.dot(x, wub_ref[...], preferred_element_type=F32) + bu_ref[...]
    gate = jnp.minimum(gate, SWIGLU_LIMIT)
    up = jnp.clip(up, -SWIGLU_LIMIT, SWIGLU_LIMIT)
    act = gate * jax.nn.sigmoid(SWIGLU_ALPHA * gate) * (up + 1.0)
    o_ref[...] = act.astype(o_ref.dtype)


def _moe_dn_kernel(be_ref, first_ref, a_ref, w_ref, b_ref, rw_ref, o_ref, wb_ref):
    it = pl.program_id(0)

    @pl.when(first_ref[it] == 1)
    def _():
        wb_ref[...] = w_ref[...].astype(BF16)

    y = jnp.dot(a_ref[...], wb_ref[...], preferred_element_type=F32) + b_ref[...]
    o_ref[...] = (y * rw_ref[...]).astype(o_ref.dtype)


def _moe_ffn(xs, row_w, block_e, first, w_gu, b_gu, w_dn, b_dn, layer, *, bn=512):
    cap, d = xs.shape
    n_items = cap // MOE_ROWS
    nj = D_EXPERT // bn
    act = pl.pallas_call(
        _moe_gu_kernel,
        grid_spec=pltpu.PrefetchScalarGridSpec(
            num_scalar_prefetch=2,
            grid=(nj, n_items),
            in_specs=[pl.BlockSpec((MOE_ROWS, d), lambda j, i, be, fi: (i, 0)),
                      pl.BlockSpec((None, None, d, bn), lambda j, i, be, fi: (layer, be[i], 0, j)),
                      pl.BlockSpec((None, None, d, bn), lambda j, i, be, fi: (layer, be[i], 0, j + nj)),
                      pl.BlockSpec((None, None, 1, bn), lambda j, i, be, fi: (layer, be[i], 0, j)),
                      pl.BlockSpec((None, None, 1, bn), lambda j, i, be, fi: (layer, be[i], 0, j + nj))],
            out_specs=pl.BlockSpec((MOE_ROWS, bn), lambda j, i, be, fi: (i, j)),
            scratch_shapes=[pltpu.VMEM((d, bn), BF16), pltpu.VMEM((d, bn), BF16)]),
        out_shape=jax.ShapeDtypeStruct((cap, D_EXPERT), BF16),
        compiler_params=_params("arbitrary", "arbitrary"),
        name="moe_gate_up",
    )(block_e, first, xs, w_gu, w_gu, b_gu.reshape(DEPTH, N_EXPERTS, 1, 2 * D_EXPERT),
      b_gu.reshape(DEPTH, N_EXPERTS, 1, 2 * D_EXPERT))
    return pl.pallas_call(
        _moe_dn_kernel,
        grid_spec=pltpu.PrefetchScalarGridSpec(
            num_scalar_prefetch=2,
            grid=(n_items,),
            in_specs=[pl.BlockSpec((MOE_ROWS, D_EXPERT), lambda i, be, fi: (i, 0)),
                      pl.BlockSpec((None, None, D_EXPERT, d), lambda i, be, fi: (layer, be[i], 0, 0)),
                      pl.BlockSpec((None, None, 1, d), lambda i, be, fi: (layer, be[i], 0, 0)),
                      pl.BlockSpec((MOE_ROWS, 1), lambda i, be, fi: (i, 0))],
            out_specs=pl.BlockSpec((MOE_ROWS, d), lambda i, be, fi: (i, 0)),
            scratch_shapes=[pltpu.VMEM((D_EXPERT, d), BF16)]),
        out_shape=jax.ShapeDtypeStruct((cap, d), F32),
        compiler_params=_params("arbitrary"),
        name="moe_down",
    )(block_e, first, act, w_dn, b_dn.reshape(DEPTH, N_EXPERTS, 1, d), row_w.reshape(cap, 1))


def _moe(h, w_router, b_router, w_gu, b_gu, w_dn, b_dn, layer):
    n, d = h.shape
    logits = jnp.dot(h, w_router[layer], precision=lax.Precision.HIGHEST) + b_router[layer]
    top_val, top_idx = lax.top_k(logits, TOP_K)
    gates = jax.nn.softmax(top_val, axis=-1)
    flat_e = top_idx.reshape(-1)
    order = jnp.argsort(flat_e)
    sorted_e = flat_e[order]
    tok = (order // TOP_K).astype(jnp.int32)
    w_sorted = gates.reshape(-1)[order]
    counts = jnp.bincount(flat_e, length=N_EXPERTS)
    padded = (counts + MOE_ROWS - 1) // MOE_ROWS * MOE_ROWS
    pad_end = jnp.cumsum(padded)
    pad_start = pad_end - padded
    grp_start = jnp.cumsum(counts) - counts
    rank = jnp.arange(n * TOP_K) - grp_start[sorted_e]
    dest = pad_start[sorted_e] + rank
    n_items = -(-(n * TOP_K) // MOE_ROWS) + N_EXPERTS
    cap = n_items * MOE_ROWS
    row_tok = jnp.zeros((cap,), jnp.int32).at[dest].set(tok)
    row_w = jnp.zeros((cap,), F32).at[dest].set(w_sorted)
    block_e = jnp.minimum(jnp.searchsorted(pad_end, jnp.arange(n_items) * MOE_ROWS, side='right'),
                          N_EXPERTS - 1).astype(jnp.int32)
    first = jnp.concatenate([jnp.ones((1,), jnp.int32),
                             (block_e[1:] != block_e[:-1]).astype(jnp.int32)])
    xs = h.astype(BF16)[row_tok]
    ys = _moe_ffn(xs, row_w, block_e, first, w_gu, b_gu, w_dn, b_dn, layer)
    return jnp.zeros((n, d), F32).at[row_tok].add(ys)


def _rmsnorm(x, g):
    return x * lax.rsqrt(jnp.mean(x * x, axis=-1, keepdims=True) + EPS) * g


def _layernorm(x, g, b):
    mu = jnp.mean(x, axis=-1, keepdims=True)
    var = jnp.mean(jnp.square(x - mu), axis=-1, keepdims=True)
    return (x - mu) * lax.rsqrt(var + EPS) * g + b


def _depthwise_conv(x, w, b):
    k = w.shape[0]
    y = lax.conv_general_dilated(x[None], w[:, None, :], (1,), [((k - 1) // 2, k // 2)],
                                 dimension_numbers=('NWC', 'WIO', 'NWC'),
                                 feature_group_count=x.shape[-1], precision=lax.Precision.HIGHEST)[0]
    return y + b


def _rope_tables(n_lat):
    rows = n_lat // GRID_W
    row = jnp.repeat(jnp.arange(rows, dtype=F32), GRID_W)
    col = jnp.tile(jnp.arange(GRID_W, dtype=F32), rows)
    half = DA_QK_DIM // 2
    inv = ROPE_BASE ** (-jnp.arange(0, half, 2, dtype=F32) / half)
    ar = row[:, None] * inv
    ac = col[:, None] * inv
    cos = jnp.concatenate([jnp.cos(ar), jnp.cos(ar), jnp.cos(ac), jnp.cos(ac)], axis=-1)
    sin = jnp.concatenate([-jnp.sin(ar), jnp.sin(ar), -jnp.sin(ac), jnp.sin(ac)], axis=-1)
    return cos, sin


def _rope(x, cos, sin):
    q = DA_QK_DIM // 4
    xr = x.reshape(x.shape[0], x.shape[1], 2, 2, q)
    partner = xr[:, :, :, ::-1, :].reshape(x.shape)
    return x * cos[:, None, :] + partner * sin[:, None, :]


def _hyena_filters(n, p):
    t = jnp.linspace(0.0, 1.0, n, dtype=F32)[:, None]
    w = 2.0 * math.pi * jnp.arange(n, dtype=F32)[:, None] / n
    bands = jnp.linspace(1e-4, HY_BANDS - 1, HY_BANDS, dtype=F32)[None, :]
    z = jnp.concatenate([t, jnp.cos(bands * w), -jnp.sin(bands * w)], axis=-1)
    hp = lax.Precision.HIGHEST
    freq = p['hy_freq']
    h = jnp.sin(freq * (jnp.dot(z, p['hy_w1'], precision=hp) + p['hy_b1']))
    h = jnp.sin(freq * (jnp.dot(h, p['hy_w2'], precision=hp) + p['hy_b2']))
    h = jnp.dot(h, p['hy_w3'], precision=hp).reshape(n, HY_ORDER, 2, HY_WIDTH)
    deltas = jnp.abs(jnp.linspace(HY_MIN_DECAY, HY_MAX_DECAY, HY_WIDTH, dtype=F32))
    h = h * jnp.exp(-t * deltas)[:, None, None, :]
    h = h / jnp.sum(jnp.abs(h), axis=(0, 2), keepdims=True)
    return jnp.concatenate([h[:, :, 0], h[::-1, :, 1]], axis=0)


def _long_conv(z, kf, bias):
    n = z.shape[0]
    zf = jnp.fft.rfft(z, n=2 * n, axis=0)
    kff = jnp.fft.rfft(kf, n=2 * n, axis=0)
    y = jnp.fft.irfft(zf * kff, n=2 * n, axis=0)[:n]
    return y + z * bias


def _hyena_branch(u, p):
    u = _depthwise_conv(u, p['hy_short_w'], p['hy_short_b'])
    x1, x2, v = jnp.split(u, 3, axis=-1)
    kf = _hyena_filters(u.shape[0], p)
    z = x1 * _long_conv(v, kf[:, 0], p['hy_bias'][0])
    return x2 * _long_conv(z, kf[:, 1], p['hy_bias'][1])


def _conv_branch(u, p):
    a, g = jnp.split(u, 2, axis=-1)
    y = a * jax.nn.sigmoid(g)
    y = _depthwise_conv(y, p['cv_dw_w'], p['cv_dw_b'])
    return jax.nn.silu(_layernorm(y, p['cv_ln_g'], p['cv_ln_b']))


def _branches_and_merge(proj, a, p, W, layer):
    hy = _hyena_branch(proj[:, OFF_HY:OFF_CV], p)
    cv = _conv_branch(proj[:, OFF_CV:OFF_GATE], p)
    g = jax.nn.sigmoid(proj[:, OFF_GATE:] + p['b_gate']).reshape(-1, N_BRANCH, D_MODEL)
    y = (g[:, 0] * _mm(a, W['w_da_out'], layer) + g[:, 1] * _mm(hy, W['w_hy_out'], layer)
         + g[:, 2] * _mm(cv, W['w_cv_out'], layer))
    return _mm(y, W['w_out'], layer)


def _token_mixer(h_lat, h_ctx, p, W, layer, need_ctx):
    s = h_lat.shape[0]
    lc = h_ctx.shape[0]
    lam_init = 0.8 - 0.6 * math.exp(-0.3 * layer)
    proj = _mm(h_lat.astype(BF16), W['w_in'], layer)
    cos, sin = _rope_tables(s)
    q = _rope(proj[:, OFF_Q:OFF_K].reshape(s, 2 * DA_HEADS, DA_QK_DIM), cos, sin).reshape(s, Q_COLS)
    k = _rope(proj[:, OFF_K:OFF_V].reshape(s, 2 * DA_HEADS, DA_QK_DIM), cos, sin).reshape(s, K_COLS)
    v = proj[:, OFF_V:OFF_HY]
    if need_ctx:
        proj_c = _mm(h_ctx.astype(BF16), W['w_in'], layer)
        k_c = proj_c[:, OFF_K:OFF_V]
        v_c = proj_c[:, OFF_V:OFF_HY]
    else:
        kv_c = _mm(h_ctx.astype(BF16), W['w_in'], layer, col_off=OFF_K, n_cols=K_COLS + V_COLS)
        k_c = kv_c[:, :K_COLS]
        v_c = kv_c[:, K_COLS:]
    kt_all = jnp.concatenate([k_c, k], axis=0).astype(BF16).T
    v_all = jnp.concatenate([v_c, v], axis=0).astype(BF16)
    a_lat = _diff_attention((q * DA_SCALE).astype(BF16), kt_all, v_all, W['da_lambda'], W['da_subln_g'],
                            layer, lam_init)
    y_lat = _branches_and_merge(proj, a_lat, p, W, layer)
    if not need_ctx:
        return y_lat, None
    q_c = proj_c[:, OFF_Q:OFF_K]
    a_ctx = _diff_attention((q_c * DA_SCALE).astype(BF16), k_c.astype(BF16).T, v_c.astype(BF16),
                            W['da_lambda'], W['da_subln_g'], layer, lam_init)
    y_ctx = _branches_and_merge(proj_c, a_ctx, p, W, layer)
    return y_lat, y_ctx


def _trunk_layer(x, ctx, c, c_ctx, p, W, layer, need_ctx):
    hp = lax.Precision.HIGHEST
    mod = jnp.dot(jax.nn.silu(c), p['w_ada'], precision=hp) + p['b_ada']
    mod_c = jnp.dot(jax.nn.silu(c_ctx), p['w_ada'], precision=hp) + p['b_ada']
    sh1, sc1, g1, sh2, sc2, g2 = jnp.split(mod, 6, axis=-1)
    csh1, csc1, cg1, csh2, csc2, cg2 = jnp.split(mod_c, 6, axis=-1)
    ng = p['norm_g']
    h_lat = _rmsnorm(x, ng[0]) * (1.0 + sc1) + sh1
    h_ctx = _rmsnorm(ctx, ng[0]) * (1.0 + csc1) + csh1
    m_lat, m_ctx = _token_mixer(h_lat, h_ctx, p, W, layer, need_ctx)
    x = x + g1 * _rmsnorm(m_lat, ng[1])
    f_lat_in = _rmsnorm(x, ng[2]) * (1.0 + sc2) + sh2
    moe = functools.partial(_moe, w_router=W['w_router'], b_router=W['b_router'], w_gu=W['w_gu'],
                            b_gu=W['b_gu'], w_dn=W['w_dn'], b_dn=W['b_dn'], layer=layer)
    if need_ctx:
        ctx = ctx + cg1 * _rmsnorm(m_ctx, ng[1])
        f_ctx_in = _rmsnorm(ctx, ng[2]) * (1.0 + csc2) + csh2
        f = moe(jnp.concatenate([f_ctx_in, f_lat_in], axis=0))
        n_c = ctx.shape[0]
        ctx = ctx + cg2 * _rmsnorm(f[:n_c], ng[3])
        f_lat = f[n_c:]
    else:
        f_lat = moe(f_lat_in)
    x = x + g2 * _rmsnorm(f_lat, ng[3])
    return x, ctx


def kernel(x, c, ctx, c_ctx, w_ada, b_ada, norm_g, w_in, b_gate, da_lambda, da_subln_g, w_da_out, hy_short_w, hy_short_b, hy_w1, hy_b1, hy_w2, hy_b2, hy_freq, hy_w3, hy_bias, w_hy_out, cv_dw_w, cv_dw_b, cv_ln_g, cv_ln_b, w_cv_out, w_out, w_router, b_router, w_gu, b_gu, w_dn, b_dn):
    W = dict(w_in=w_in, da_lambda=da_lambda, da_subln_g=da_subln_g, w_da_out=w_da_out, w_hy_out=w_hy_out,
             w_cv_out=w_cv_out, w_out=w_out, w_router=w_router, b_router=b_router, w_gu=w_gu, b_gu=b_gu,
             w_dn=w_dn, b_dn=b_dn)
    xl = x[0]
    cl = ctx[0]
    for layer in range(DEPTH):
        p = {
            'w_ada': w_ada[layer], 'b_ada': b_ada[layer], 'norm_g': norm_g[layer], 'b_gate': b_gate[layer],
            'hy_short_w': hy_short_w[layer], 'hy_short_b': hy_short_b[layer],
            'hy_w1': hy_w1[layer], 'hy_b1': hy_b1[layer], 'hy_w2': hy_w2[layer], 'hy_b2': hy_b2[layer],
            'hy_freq': hy_freq[layer], 'hy_w3': hy_w3[layer], 'hy_bias': hy_bias[layer],
            'cv_dw_w': cv_dw_w[layer], 'cv_dw_b': cv_dw_b[layer], 'cv_ln_g': cv_ln_g[layer],
            'cv_ln_b': cv_ln_b[layer],
        }
        xl, cl = _trunk_layer(xl, cl, c, c_ctx, p, W, layer, layer < DEPTH - 1)
    return xl[None]
```

```python
import functools
import math

import numpy as np
import jax
import jax.numpy as jnp
from jax import lax
from jax.experimental import pallas as pl
from jax.experimental.pallas import tpu as pltpu

F32 = jnp.float32
BF16 = jnp.bfloat16

D_MODEL = 2048
DEPTH = 2
GRID_W = 64
EPS = 1e-6

DA_HEADS = 8
DA_QK_DIM = 64
DA_V_DIM = 2 * DA_QK_DIM
DA_WIDTH = DA_HEADS * DA_V_DIM
DA_SCALE = DA_QK_DIM ** -0.5
ROPE_BASE = 10000.0

HY_WIDTH = 1024
HY_ORDER = 2
HY_SHORT = 3
HY_EMB = 33
HY_BANDS = (HY_EMB - 1) // 2
HY_FFN = 64
HY_MIN_DECAY = math.log(1e-2) / 1.5
HY_MAX_DECAY = math.log(1e-2) / 0.3

CV_WIDTH = 1024
CV_KERNEL = 31
N_BRANCH = 3

N_EXPERTS = 32
TOP_K = 4
D_EXPERT = 1024
SWIGLU_LIMIT = 7.0
SWIGLU_ALPHA = 1.702

Q_COLS = 2 * DA_HEADS * DA_QK_DIM
K_COLS = Q_COLS
V_COLS = DA_WIDTH
HY_COLS = (HY_ORDER + 1) * HY_WIDTH
CV_COLS = 2 * CV_WIDTH
GATE_COLS = N_BRANCH * D_MODEL
OFF_Q = 0
OFF_K = OFF_Q + Q_COLS
OFF_V = OFF_K + K_COLS
OFF_HY = OFF_V + V_COLS
OFF_CV = OFF_HY + HY_COLS
OFF_GATE = OFF_CV + CV_COLS
IN_COLS = OFF_GATE + GATE_COLS

LANES = 128
VMEM_LIMIT_BYTES = 56 * 1024 * 1024
MOE_ROWS = 256
FFT_R = 128
FFT_C = 128
CONV_HALO = 16


def _params(*sem):
    return pltpu.CompilerParams(dimension_semantics=sem, vmem_limit_bytes=VMEM_LIMIT_BYTES)


def _once(shape, index_map):
    return pl.BlockSpec(shape, index_map, pipeline_mode=pl.Buffered(1))


def _split_bf16(a):
    hi = a.astype(BF16)
    return hi, (a - hi.astype(F32)).astype(BF16)


def _dot3(a, b):
    ah, al = _split_bf16(a)
    bh, bl = _split_bf16(b)
    d = functools.partial(jnp.dot, preferred_element_type=F32)
    return d(ah, bh) + (d(ah, bl) + d(al, bh))


def _rms(x):
    return x * lax.rsqrt(jnp.mean(x * x, axis=-1, keepdims=True) + EPS)


def _mm_kernel(x_ref, w_ref, o_ref, wb_ref):
    @pl.when(pl.program_id(1) == 0)
    def _():
        wb_ref[...] = w_ref[...].astype(BF16)

    o_ref[...] = jnp.dot(x_ref[...].astype(BF16), wb_ref[...],
                         preferred_element_type=F32).astype(o_ref.dtype)


def _mm(x, w, layer, *, col_off=0, n_cols=None, out_dtype=F32, bm=1024, bn=1024):
    m, k = x.shape
    n = w.shape[2] if n_cols is None else n_cols
    bm = min(bm, m)
    bn = min(bn, n)
    assert m % bm == 0 and n % bn == 0 and col_off % bn == 0
    joff = col_off // bn
    return pl.pallas_call(
        _mm_kernel,
        grid=(n // bn, m // bm),
        in_specs=[pl.BlockSpec((bm, k), lambda j, i: (i, 0)),
                  pl.BlockSpec((None, k, bn), lambda j, i: (layer, 0, j + joff))],
        out_specs=pl.BlockSpec((bm, bn), lambda j, i: (i, j)),
        out_shape=jax.ShapeDtypeStruct((m, n), out_dtype),
        scratch_shapes=[pltpu.VMEM((k, bn), BF16)],
        compiler_params=_params("arbitrary", "arbitrary"),
        name="dense_mm",
    )(x, w)


def _ada_kernel(c_ref, w_ref, b_ref, o_ref):
    c = c_ref[...]
    o_ref[...] = _dot3(c * jax.nn.sigmoid(c), w_ref[...]) + b_ref[...]


def _ada(c8, w_ada, b_ada, layer, *, bn=1024):
    r, d = c8.shape
    n = w_ada.shape[2]
    return pl.pallas_call(
        _ada_kernel,
        grid=(n // bn,),
        in_specs=[pl.BlockSpec((r, d), lambda j: (0, 0)),
                  pl.BlockSpec((None, d, bn), lambda j: (layer, 0, j)),
                  pl.BlockSpec((None, 1, bn), lambda j: (layer, 0, j))],
        out_specs=pl.BlockSpec((r, bn), lambda j: (0, j)),
        out_shape=jax.ShapeDtypeStruct((r, n), F32),
        compiler_params=_params("arbitrary"),
        name="ada_mod",
    )(c8, w_ada, b_ada.reshape(DEPTH, 1, n))


def _norm_mod_kernel(x_ref, g_ref, sc_ref, sh_ref, o_ref):
    y = _rms(x_ref[...]) * g_ref[...]
    o_ref[...] = (y * (1.0 + sc_ref[...]) + sh_ref[...]).astype(o_ref.dtype)


def _vec(idx):
    return pl.BlockSpec((None, 1, D_MODEL), lambda *_: (idx, 0, 0))


def _norm_mod(x, ng3, mod3, ng_idx, sc_idx, sh_idx, *, bm=512):
    m, d = x.shape
    bm = min(bm, m)
    return pl.pallas_call(
        _norm_mod_kernel,
        grid=(m // bm,),
        in_specs=[pl.BlockSpec((bm, d), lambda i: (i, 0)), _vec(ng_idx), _vec(sc_idx), _vec(sh_idx)],
        out_specs=pl.BlockSpec((bm, d), lambda i: (i, 0)),
        out_shape=jax.ShapeDtypeStruct((m, d), BF16),
        compiler_params=_params("arbitrary"),
        name="norm_mod",
    )(x, ng3, mod3, mod3)


def _resid_kernel(x_ref, f_ref, ng_ref, gate_ref, o_ref):
    o_ref[...] = x_ref[...] + gate_ref[...] * (_rms(f_ref[...]) * ng_ref[...])


def _resid_norm(x, f, ng3, mod3, ng_idx, gate_idx, *, row_off=0, bm=256):
    m, d = x.shape
    bm = min(bm, m)
    assert row_off % bm == 0
    roff = row_off // bm
    return pl.pallas_call(
        _resid_kernel,
        grid=(m // bm,),
        in_specs=[pl.BlockSpec((bm, d), lambda i: (i, 0)), pl.BlockSpec((bm, d), lambda i: (i + roff, 0)),
                  _vec(ng_idx), _vec(gate_idx)],
        out_specs=pl.BlockSpec((bm, d), lambda i: (i, 0)),
        out_shape=jax.ShapeDtypeStruct((m, d), F32),
        compiler_params=_params("arbitrary"),
        name="resid_norm",
    )(x, f, ng3, mod3)


def _rope_tile(x, cos, sa, sb):
    q4 = DA_QK_DIM // 4
    return x * cos + pltpu.roll(x, LANES - q4, 1) * sa + pltpu.roll(x, q4, 1) * sb


def _q_prep_kernel(p_ref, cos_ref, sa_ref, sb_ref, o_ref, *, rope):
    for h in range(DA_HEADS):
        sl = slice(h * LANES, (h + 1) * LANES)
        x = p_ref[:, sl]
        if rope:
            x = _rope_tile(x, cos_ref[...], sa_ref[...], sb_ref[...])
        o_ref[:, sl] = (x * DA_SCALE).astype(BF16)


def _q_prep(proj, tabs, *, rope, bm=256):
    m = proj.shape[0]
    bm = min(bm, m)
    tab = pl.BlockSpec((bm, LANES), lambda i: (i, 0))
    return pl.pallas_call(
        functools.partial(_q_prep_kernel, rope=rope),
        grid=(m // bm,),
        in_specs=[pl.BlockSpec((bm, Q_COLS), lambda i: (i, OFF_Q // Q_COLS)), tab, tab, tab],
        out_specs=pl.BlockSpec((bm, Q_COLS), lambda i: (i, 0)),
        out_shape=jax.ShapeDtypeStruct((m, Q_COLS), BF16),
        compiler_params=_params("arbitrary"),
        name="q_prep",
    )(proj, *tabs)


def _kv_prep_kernel(kc_ref, vc_ref, k_ref, v_ref, cos_ref, sa_ref, sb_ref, kt_ref, vo_ref):
    i = pl.program_id(0)

    @pl.when(i == 0)
    def _():
        for h in range(DA_HEADS):
            sl = slice(h * LANES, (h + 1) * LANES)
            kt_ref[sl, :] = kc_ref[:, sl].T.astype(BF16)
        vo_ref[...] = vc_ref[...].astype(BF16)

    @pl.when(i > 0)
    def _():
        for h in range(DA_HEADS):
            sl = slice(h * LANES, (h + 1) * LANES)
            kt_ref[sl, :] = _rope_tile(k_ref[:, sl], cos_ref[...], sa_ref[...], sb_ref[...]).T.astype(BF16)
        vo_ref[...] = v_ref[...].astype(BF16)


def _kv_prep(proj, ctx_kv, ctx_k_blk, tabs):
    s = proj.shape[0]
    lc = ctx_kv.shape[0]
    assert s % lc == 0
    nb = s // lc + 1
    lat = lambda i: (jnp.maximum(i - 1, 0), 0)
    tab = pl.BlockSpec((lc, LANES), lat)
    return pl.pallas_call(
        _kv_prep_kernel,
        grid=(nb,),
        in_specs=[pl.BlockSpec((lc, K_COLS), lambda i: (0, ctx_k_blk)),
                  pl.BlockSpec((lc, V_COLS), lambda i: (0, ctx_k_blk + 1)),
                  pl.BlockSpec((lc, K_COLS), lambda i: (jnp.maximum(i - 1, 0), OFF_K // K_COLS)),
                  pl.BlockSpec((lc, V_COLS), lambda i: (jnp.maximum(i - 1, 0), OFF_V // V_COLS)),
                  tab, tab, tab],
        out_specs=[pl.BlockSpec((K_COLS, lc), lambda i: (0, i)),
                   pl.BlockSpec((lc, V_COLS), lambda i: (i, 0))],
        out_shape=[jax.ShapeDtypeStruct((K_COLS, s + lc), BF16),
                   jax.ShapeDtypeStruct((s + lc, V_COLS), BF16)],
        compiler_params=_params("arbitrary"),
        name="kv_prep",
    )(ctx_kv, ctx_kv, proj, proj, *tabs)


def _rope_tables(n_lat):
    rows = n_lat // GRID_W
    row = jnp.repeat(jnp.arange(rows, dtype=F32), GRID_W)
    col = jnp.tile(jnp.arange(GRID_W, dtype=F32), rows)
    half = DA_QK_DIM // 2
    inv = ROPE_BASE ** (-jnp.arange(0, half, 2, dtype=F32) / half)
    ar = row[:, None] * inv
    ac = col[:, None] * inv
    zr = jnp.zeros_like(ar)
    cos = jnp.concatenate([jnp.cos(ar), jnp.cos(ar), jnp.cos(ac), jnp.cos(ac)], axis=-1)
    sa = jnp.concatenate([-jnp.sin(ar), zr, -jnp.sin(ac), zr], axis=-1)
    sb = jnp.concatenate([zr, jnp.sin(ar), zr, jnp.sin(ac)], axis=-1)
    return tuple(jnp.tile(t, (1, LANES // DA_QK_DIM)) for t in (cos, sa, sb))


def _attn_kernel(lp_ref, g_ref, q_ref, kt_ref, v_ref, o_ref, *, bk, lam_init):
    bq = q_ref.shape[0]
    nk = kt_ref.shape[1] // bk
    q = q_ref[...]
    lane = lax.broadcasted_iota(jnp.int32, q.shape, 1)
    zero = jnp.zeros_like(q)
    q2 = jnp.concatenate([jnp.where(lane < DA_QK_DIM, q, zero),
                          jnp.where(lane >= DA_QK_DIM, q, zero)], axis=0)

    def body(i, carry):
        m, l, acc = carry
        off = pl.multiple_of(i * bk, bk)
        s = jnp.dot(q2, kt_ref[:, pl.ds(off, bk)], preferred_element_type=F32)
        m_new = jnp.maximum(m, jnp.max(s, axis=-1, keepdims=True))
        alpha = jnp.exp(m - m_new)
        p = jnp.exp(s - m_new)
        l = alpha * l + jnp.sum(p, axis=-1, keepdims=True)
        acc = alpha * acc + jnp.dot(p.astype(BF16), v_ref[pl.ds(off, bk), :],
                                    preferred_element_type=F32)
        return m_new, l, acc

    m0 = jnp.full((2 * bq, 1), -jnp.inf, F32)
    l0 = jnp.zeros((2 * bq, 1), F32)
    a0 = jnp.zeros((2 * bq, DA_V_DIM), F32)
    _, l, acc = lax.fori_loop(0, nk, body, (m0, l0, a0))
    o = acc / l
    lp = lp_ref[...].astype(F32)
    lam = (jnp.exp(jnp.sum(lp[0:1] * lp[1:2], axis=-1, keepdims=True))
           - jnp.exp(jnp.sum(lp[2:3] * lp[3:4], axis=-1, keepdims=True)) + lam_init)
    o = o[:bq] - lam * o[bq:]
    y = _rms(o) * g_ref[...].astype(F32) * (1.0 - lam_init)
    o_ref[...] = y.astype(o_ref.dtype)


def _diff_attention(q, kt, v, da_lambda, da_subln_g, layer, lam_init, *, bq=256, bk=None):
    sq = q.shape[0]
    sk = v.shape[0]
    bq = min(bq, sq)
    if bk is None:
        bk = 768 if sk % 768 == 0 else sk
    hw = 2 * DA_QK_DIM
    return pl.pallas_call(
        functools.partial(_attn_kernel, bk=bk, lam_init=lam_init),
        grid=(DA_HEADS, sq // bq),
        in_specs=[pl.BlockSpec((None, 4, DA_QK_DIM), lambda h, i: (layer, 0, 0)),
                  pl.BlockSpec((None, 1, DA_V_DIM), lambda h, i: (layer, 0, 0)),
                  pl.BlockSpec((bq, hw), lambda h, i: (i, h)),
                  pl.BlockSpec((hw, sk), lambda h, i: (h, 0)),
                  pl.BlockSpec((sk, DA_V_DIM), lambda h, i: (0, h))],
        out_specs=pl.BlockSpec((bq, DA_V_DIM), lambda h, i: (i, h)),
        out_shape=jax.ShapeDtypeStruct((sq, DA_WIDTH), BF16),
        compiler_params=_params("arbitrary", "arbitrary"),
        name="diff_attention",
    )(da_lambda, da_subln_g.reshape(DEPTH, 1, DA_V_DIM), q, kt, v)


def _dwconv_kernel(*refs, ntaps, glu, n_row_blocks):
    if glu:
        cur, prv, nxt, gcur, gprv, gnxt, w_ref, b_ref, o_ref, pad_ref = refs
    else:
        cur, prv, nxt, w_ref, b_ref, o_ref, pad_ref = refs
        gcur = gprv = gnxt = None
    i = pl.program_id(1)
    t, cw = cur.shape
    h = CONV_HALO
    half = (ntaps - 1) // 2

    def val(a, g):
        return a[...] * jax.nn.sigmoid(g[...]) if glu else a[...]

    pad_ref[pl.ds(h, t), :] = val(cur, gcur)
    pad_ref[pl.ds(0, h), :] = jnp.where(i > 0, val(prv, gprv), 0.0)
    pad_ref[pl.ds(h + t, h), :] = jnp.where(i < n_row_blocks - 1, val(nxt, gnxt), 0.0)
    rc = 64
    for r in range(0, t, rc):
        for c in range(0, cw, LANES):
            acc = jnp.broadcast_to(b_ref[:, c:c + LANES], (rc, LANES))
            for j in range(ntaps):
                acc = acc + w_ref[j:j + 1, c:c + LANES] * pad_ref[pl.ds(r + h - half + j, rc), c:c + LANES]
            o_ref[pl.ds(r, rc), c:c + LANES] = acc


def _dwconv(u, col_off, n_ch, w, b, layer, *, glu=False, t=512, cw=256):
    s = u.shape[0]
    ntaps = w.shape[1]
    t = min(t, s)
    assert s % t == 0 and n_ch % cw == 0 and col_off % cw == 0 and t % CONV_HALO == 0
    nrb = s // t
    hb = t // CONV_HALO
    last_h = s // CONV_HALO - 1
    coff = col_off // cw
    goff = (col_off + n_ch) // cw

    def specs(off):
        return [pl.BlockSpec((t, cw), lambda c, i: (i, c + off)),
                pl.BlockSpec((CONV_HALO, cw), lambda c, i: (jnp.maximum(i * hb - 1, 0), c + off)),
                pl.BlockSpec((CONV_HALO, cw), lambda c, i: (jnp.minimum((i + 1) * hb, last_h), c + off))]

    in_specs = specs(coff) + (specs(goff) if glu else [])
    args = [u] * (6 if glu else 3)
    in_specs += [pl.BlockSpec((None, ntaps, cw), lambda c, i: (layer, 0, c)),
                 pl.BlockSpec((None, 1, cw), lambda c, i: (layer, 0, c))]
    return pl.pallas_call(
        functools.partial(_dwconv_kernel, ntaps=ntaps, glu=glu, n_row_blocks=nrb),
        grid=(n_ch // cw, nrb),
        in_specs=in_specs,
        out_specs=pl.BlockSpec((t, cw), lambda c, i: (i, c)),
        out_shape=jax.ShapeDtypeStruct((s, n_ch), F32),
        scratch_shapes=[pltpu.VMEM((t + 2 * CONV_HALO, cw), F32)],
        compiler_params=_params("arbitrary", "arbitrary"),
        name="dwconv",
    )(*args, w, b.reshape(DEPTH, 1, n_ch))


def _hy_filter_kernel(z_ref, w1_ref, b1_ref, w2_ref, b2_ref, fr_ref, w3_ref, dl_ref, h_ref, s_ref):
    z = z_ref[...]
    fr = fr_ref[...]
    a = jnp.sin(fr * (_dot3(z, w1_ref[...]) + b1_ref[...]))
    a = jnp.sin(fr * (_dot3(a, w2_ref[...]) + b2_ref[...]))
    hh = _dot3(a, w3_ref[...]) * jnp.exp(-z[:, 0:1] * dl_ref[...])
    h_ref[...] = hh

    @pl.when(pl.program_id(0) == 0)
    def _():
        s_ref[...] = jnp.zeros_like(s_ref)

    s_ref[...] += jnp.sum(jnp.abs(hh), axis=0, keepdims=True)


def _hy_filters(n, hy_w1, hy_b1, hy_w2, hy_b2, hy_freq, hy_w3, layer, *, t=512):
    t = min(t, n)
    tt = jnp.linspace(0.0, 1.0, n, dtype=F32)[:, None]
    w = 2.0 * math.pi * jnp.arange(n, dtype=F32)[:, None] / n
    bands = jnp.linspace(1e-4, HY_BANDS - 1, HY_BANDS, dtype=F32)[None, :]
    z = jnp.concatenate([tt, jnp.cos(bands * w), -jnp.sin(bands * w),
                         jnp.zeros((n, LANES - HY_EMB), F32)], axis=-1)
    w1 = jnp.pad(hy_w1, ((0, 0), (0, LANES - HY_EMB), (0, 0)))
    deltas = jnp.abs(jnp.linspace(HY_MIN_DECAY, HY_MAX_DECAY, HY_WIDTH, dtype=F32))
    nf = HY_ORDER * 2 * HY_WIDTH
    dl = jnp.tile(deltas, HY_ORDER * 2)[None, :]
    row = lambda a: a.reshape(DEPTH, 1, HY_FFN)
    small = pl.BlockSpec((None, 1, HY_FFN), lambda i: (layer, 0, 0))
    return pl.pallas_call(
        _hy_filter_kernel,
        grid=(n // t,),
        in_specs=[pl.BlockSpec((t, LANES), lambda i: (i, 0)),
                  pl.BlockSpec((None, LANES, HY_FFN), lambda i: (layer, 0, 0)), small,
                  pl.BlockSpec((None, HY_FFN, HY_FFN), lambda i: (layer, 0, 0)), small, small,
                  pl.BlockSpec((None, HY_FFN, nf), lambda i: (layer, 0, 0)),
                  pl.BlockSpec((1, nf), lambda i: (0, 0))],
        out_specs=[pl.BlockSpec((t, nf), lambda i: (i, 0)), pl.BlockSpec((1, nf), lambda i: (0, 0))],
        out_shape=[jax.ShapeDtypeStruct((n, nf), F32), jax.ShapeDtypeStruct((1, nf), F32)],
        compiler_params=_params("arbitrary"),
        name="hy_filters",
    )(z, w1, row(hy_b1), hy_w2, row(hy_b2), row(hy_freq), hy_w3, dl)


def _angles(num, den):
    return 2.0 * np.pi * (np.asarray(num, np.int64) % den).astype(np.float64) / den


@functools.lru_cache(maxsize=None)
def _fft_tables():
    r = FFT_R
    n_fft = r * r
    hr = r // 2
    s2 = np.arange(r)[:, None, None]
    f1 = np.arange(r)[None, :, None]
    s1 = np.arange(hr)[None, None, :]
    th = _angles(f1 * (r * s1 + s2), n_fft)
    fwd = np.concatenate([np.cos(th), -np.sin(th)], axis=1)
    thb = _angles(f1 * (r * (r - 1 - s1) + s2), n_fft)
    bwd = np.concatenate([np.cos(thb), -np.sin(thb)], axis=1)
    fwd = np.concatenate([fwd, bwd], axis=2)
    t2 = _angles(np.arange(r)[:, None] * np.arange(r)[None, :], r)
    c, s = np.cos(t2), np.sin(t2)
    m2 = np.block([[c, s], [-s, c]])
    m3 = np.block([[c, -s], [s, c]])
    tt2 = np.arange(r)[:, None, None]
    tt1 = np.arange(hr)[None, :, None]
    ff1 = np.arange(r)[None, None, :]
    ph = _angles(ff1 * (r * tt1 + tt2), n_fft)
    g4 = np.concatenate([np.cos(ph), -np.sin(ph)], axis=2) / n_fft
    return tuple(jnp.asarray(a, dtype=BF16) for a in (fwd, m2, m3, g4))


def _store_stage1(a_ref, s2, out):
    r = FFT_R
    a_ref[pl.ds(s2, r, stride=2 * r), :] = out[:r]
    a_ref[pl.ds(s2 + r, r, stride=2 * r), :] = out[r:]


def _hy_spec_kernel(hf_ref, hb_ref, inv_ref, fwd_ref, m2_ref, k_ref, a_ref):
    r = FFT_R
    hr = r // 2

    def stage1(s2, c):
        xf = hf_ref[pl.ds(s2, hr, stride=r), :]
        xb = hb_ref[pl.ds(r - 1 - s2, hr, stride=r), :]
        xs = jnp.concatenate([xf, xb], axis=0).astype(BF16)
        _store_stage1(a_ref, s2, jnp.dot(fwd_ref[s2], xs, preferred_element_type=F32))
        return c

    lax.fori_loop(0, r, stage1, 0, unroll=4)

    def stage2(f1, c):
        r0 = pl.multiple_of(f1 * 2 * r, 2 * r)
        z = jnp.dot(m2_ref[...], a_ref[pl.ds(r0, 2 * r), :].astype(BF16), preferred_element_type=F32)
        k_ref[f1] = (z * inv_ref[...]).astype(k_ref.dtype)
        return c

    lax.fori_loop(0, r, stage2, 0, unroll=4)


def _hy_spectra(h, inv_norm, tables):
    n = h.shape[0]
    r = FFT_R
    assert 2 * n == r * r
    fwd, m2, _, _ = tables
    nc = HY_WIDTH // FFT_C
    return pl.pallas_call(
        _hy_spec_kernel,
        grid=(HY_ORDER, nc),
        in_specs=[_once((n, FFT_C), lambda o, c: (0, o * 2 * nc + c)),
                  _once((n, FFT_C), lambda o, c: (0, o * 2 * nc + nc + c)),
                  pl.BlockSpec((1, FFT_C), lambda o, c: (0, o * nc + c)),
                  _once((r, 2 * r, r), lambda o, c: (0, 0, 0)),
                  _once((2 * r, 2 * r), lambda o, c: (0, 0))],
        out_specs=pl.BlockSpec((None, None, r, 2 * r, FFT_C), lambda o, c: (o, c, 0, 0, 0)),
        out_shape=jax.ShapeDtypeStruct((HY_ORDER, nc, r, 2 * r, FFT_C), BF16),
        scratch_shapes=[pltpu.VMEM((2 * r * r, FFT_C), F32)],
        compiler_params=_params("arbitrary", "arbitrary"),
        name="hy_spectra",
    )(h, h, inv_norm, fwd, m2)


def _hy_conv_kernel(v_ref, x_ref, k_ref, bias_ref, fwd_ref, m2_ref, m3_ref, g4_ref, o_ref, a_ref):
    r = FFT_R
    hr = r // 2

    def stage1(s2, c):
        xs = v_ref[pl.ds(s2, hr, stride=r), :].astype(BF16)
        _store_stage1(a_ref, s2, jnp.dot(fwd_ref[s2][:, :hr], xs, preferred_element_type=F32))
        return c

    lax.fori_loop(0, r, stage1, 0, unroll=4)

    def stage23(f1, c):
        r0 = pl.multiple_of(f1 * 2 * r, 2 * r)
        z = jnp.dot(m2_ref[...], a_ref[pl.ds(r0, 2 * r), :].astype(BF16), preferred_element_type=F32)
        kk = k_ref[f1].astype(F32)
        zr, zi, kr, ki = z[:r], z[r:], kk[:r], kk[r:]
        y = jnp.concatenate([zr * kr - zi * ki, zr * ki + zi * kr], axis=0).astype(BF16)
        a_ref[pl.ds(r0, 2 * r), :] = jnp.dot(m3_ref[...], y, preferred_element_type=F32)
        return c

    lax.fori_loop(0, r, stage23, 0, unroll=4)

    def stage4(t2, c):
        b = jnp.concatenate([a_ref[pl.ds(t2, r, stride=2 * r), :],
                             a_ref[pl.ds(t2 + r, r, stride=2 * r), :]], axis=0).astype(BF16)
        y = jnp.dot(g4_ref[t2], b, preferred_element_type=F32)
        vs = v_ref[pl.ds(t2, hr, stride=r), :]
        xs = x_ref[pl.ds(t2, hr, stride=r), :]
        o_ref[pl.ds(t2, hr, stride=r), :] = (xs * (y + vs * bias_ref[...])).astype(o_ref.dtype)
        return c

    lax.fori_loop(0, r, stage4, 0, unroll=4)


def _hy_conv(zin, zin_off, xmul, xmul_off, spectra, order, hy_bias, layer, tables, out_dtype):
    n = zin.shape[0]
    r = FFT_R
    assert 2 * n == r * r and zin_off % FFT_C == 0 and xmul_off % FFT_C == 0
    fwd, m2, m3, g4 = tables
    nc = HY_WIDTH // FFT_C
    hr = r // 2
    zo = zin_off // FFT_C
    xo = xmul_off // FFT_C
    return pl.pallas_call(
        _hy_conv_kernel,
        grid=(nc,),
        in_specs=[_once((n, FFT_C), lambda c: (0, c + zo)),
                  _once((n, FFT_C), lambda c: (0, c + xo)),
                  _once((None, None, r, 2 * r, FFT_C), lambda c: (order, c, 0, 0, 0)),
                  pl.BlockSpec((None, 1, FFT_C), lambda c: (layer * HY_ORDER + order, 0, c)),
                  _once((r, 2 * r, r), lambda c: (0, 0, 0)),
                  _once((2 * r, 2 * r), lambda c: (0, 0)),
                  _once((2 * r, 2 * r), lambda c: (0, 0)),
                  _once((r, hr, 2 * r), lambda c: (0, 0, 0))],
        out_specs=pl.BlockSpec((n, FFT_C), lambda c: (0, c)),
        out_shape=jax.ShapeDtypeStruct((n, HY_WIDTH), out_dtype),
        scratch_shapes=[pltpu.VMEM((2 * r * r, FFT_C), F32)],
        compiler_params=_params("arbitrary"),
        name="hy_conv",
    )(zin, xmul, spectra, hy_bias.reshape(DEPTH * HY_ORDER, 1, HY_WIDTH), fwd, m2, m3, g4)


@functools.lru_cache(maxsize=None)
def _dft_tables(n):
    n_fft = 2 * n
    f = np.arange(n_fft)[:, None]
    s = np.arange(n)[None, :]
    th = _angles(f * s, n_fft)
    fz = np.concatenate([np.cos(th), -np.sin(th)], axis=0)
    thb = _angles(f * (n_fft - 1 - s), n_fft)
    fb = np.concatenate([np.cos(thb), -np.sin(thb)], axis=0)
    ph = _angles(np.arange(n)[:, None] * np.arange(n_fft)[None, :], n_fft)
    g = np.concatenate([np.cos(ph), -np.sin(ph)], axis=1) / n_fft
    return tuple(jnp.asarray(a, dtype=BF16) for a in (fz, fb, g))


def _hy_conv_small_kernel(v_ref, x_ref, hf_ref, hb_ref, inv_ref, bias_ref, fz_ref, fb_ref, g_ref, o_ref):
    d = functools.partial(jnp.dot, preferred_element_type=F32)
    v = v_ref[...]
    nf = fz_ref.shape[0] // 2
    z = d(fz_ref[...], v.astype(BF16))
    k = (d(fz_ref[...], hf_ref[...].astype(BF16)) + d(fb_ref[...], hb_ref[...].astype(BF16))) * inv_ref[...]
    zr, zi, kr, ki = z[:nf], z[nf:], k[:nf], k[nf:]
    y = jnp.concatenate([zr * kr - zi * ki, zr * ki + zi * kr], axis=0).astype(BF16)
    o_ref[...] = (x_ref[...] * (d(g_ref[...], y) + v * bias_ref[...])).astype(o_ref.dtype)


def _hy_conv_small(zin, zin_off, xmul, xmul_off, h, inv_norm, order, hy_bias, layer, out_dtype, *, cw=256):
    n = zin.shape[0]
    fz, fb, g = _dft_tables(n)
    nc = HY_WIDTH // cw
    zo = zin_off // cw
    xo = xmul_off // cw
    full = lambda a: pl.BlockSpec(a.shape, lambda c: (0, 0))
    return pl.pallas_call(
        _hy_conv_small_kernel,
        grid=(nc,),
        in_specs=[pl.BlockSpec((n, cw), lambda c: (0, c + zo)),
                  pl.BlockSpec((n, cw), lambda c: (0, c + xo)),
                  pl.BlockSpec((n, cw), lambda c: (0, order * 2 * nc + c)),
                  pl.BlockSpec((n, cw), lambda c: (0, order * 2 * nc + nc + c)),
                  pl.BlockSpec((1, cw), lambda c: (0, order * nc + c)),
                  pl.BlockSpec((None, 1, cw), lambda c: (layer * HY_ORDER + order, 0, c)),
                  full(fz), full(fb), full(g)],
        out_specs=pl.BlockSpec((n, cw), lambda c: (0, c)),
        out_shape=jax.ShapeDtypeStruct((n, HY_WIDTH), out_dtype),
        compiler_params=_params("arbitrary"),
        name="hy_conv_small",
    )(zin, xmul, h, h, inv_norm, hy_bias.reshape(DEPTH * HY_ORDER, 1, HY_WIDTH), fz, fb, g)


def _hyena_branch(proj, W, layer):
    n = proj.shape[0]
    u = _dwconv(proj, OFF_HY, HY_COLS, W['hy_short_w'], W['hy_short_b'], layer)
    h, hsum = _hy_filters(n, W['hy_w1'], W['hy_b1'], W['hy_w2'], W['hy_b2'], W['hy_freq'], W['hy_w3'], layer)
    inv_norm = (1.0 / jnp.sum(hsum.reshape(HY_ORDER, 2, HY_WIDTH), axis=1)).reshape(1, HY_ORDER * HY_WIDTH)
    if 2 * n == FFT_R * FFT_R:
        tables = _fft_tables()
        spectra = _hy_spectra(h, inv_norm, tables)
        z1 = _hy_conv(u, 2 * HY_WIDTH, u, 0, spectra, 0, W['hy_bias'], layer, tables, F32)
        return _hy_conv(z1, 0, u, HY_WIDTH, spectra, 1, W['hy_bias'], layer, tables, F32)
    z1 = _hy_conv_small(u, 2 * HY_WIDTH, u, 0, h, inv_norm, 0, W['hy_bias'], layer, F32)
    return _hy_conv_small(z1, 0, u, HY_WIDTH, h, inv_norm, 1, W['hy_bias'], layer, F32)


def _merge1_kernel(a_ref, hy_ref, cv_ref, g0_ref, g1_ref, g2_ref, bg0_ref, bg1_ref, bg2_ref,
                   wa_ref, wh_ref, wc_ref, lng_ref, lnb_ref, o_ref, wab_ref, whb_ref, wcb_ref):
    @pl.when(pl.program_id(1) == 0)
    def _():
        wab_ref[...] = wa_ref[...].astype(BF16)
        whb_ref[...] = wh_ref[...].astype(BF16)
        wcb_ref[...] = wc_ref[...].astype(BF16)

    d = functools.partial(jnp.dot, preferred_element_type=F32)
    cv = cv_ref[...]
    mu = jnp.mean(cv, axis=-1, keepdims=True)
    var = jnp.mean(jnp.square(cv - mu), axis=-1, keepdims=True)
    cv = (cv - mu) * lax.rsqrt(var + EPS) * lng_ref[...] + lnb_ref[...]
    cv = cv * jax.nn.sigmoid(cv)
    y = (jax.nn.sigmoid(g0_ref[...] + bg0_ref[...]) * d(a_ref[...], wab_ref[...])
         + jax.nn.sigmoid(g1_ref[...] + bg1_ref[...]) * d(hy_ref[...].astype(BF16), whb_ref[...])
         + jax.nn.sigmoid(g2_ref[...] + bg2_ref[...]) * d(cv.astype(BF16), wcb_ref[...]))
    o_ref[...] = y.astype(o_ref.dtype)


def _merge1(proj, a, hy, cvraw, W, layer, *, bm=512, bn=512):
    m = proj.shape[0]
    bm = min(bm, m)
    d = D_MODEL
    nj = d // bn
    gate = lambda b: pl.BlockSpec((bm, bn), lambda j, i: (i, (OFF_GATE + b * d) // bn + j))
    bias = lambda b: pl.BlockSpec((None, 1, bn), lambda j, i: (layer, 0, b * nj + j))
    wspec = lambda k: pl.BlockSpec((None, k, bn), lambda j, i: (layer, 0, j))
    act = lambda k: pl.BlockSpec((bm, k), lambda j, i: (i, 0))
    lnv = pl.BlockSpec((None, 1, CV_WIDTH), lambda j, i: (layer, 0, 0))
    return pl.pallas_call(
        _merge1_kernel,
        grid=(nj, m // bm),
        in_specs=[act(DA_WIDTH), act(HY_WIDTH), act(CV_WIDTH), gate(0), gate(1), gate(2),
                  bias(0), bias(1), bias(2), wspec(DA_WIDTH), wspec(HY_WIDTH), wspec(CV_WIDTH), lnv, lnv],
        out_specs=pl.BlockSpec((bm, bn), lambda j, i: (i, j)),
        out_shape=jax.ShapeDtypeStruct((m, d), BF16),
        scratch_shapes=[pltpu.VMEM((DA_WIDTH, bn), BF16), pltpu.VMEM((HY_WIDTH, bn), BF16),
                        pltpu.VMEM((CV_WIDTH, bn), BF16)],
        compiler_params=_params("arbitrary", "arbitrary"),
        name="merge_branches",
    )(a, hy, cvraw, proj, proj, proj, *([W['b_gate'].reshape(DEPTH, 1, GATE_COLS)] * 3),
      W['w_da_out'], W['w_hy_out'], W['w_cv_out'],
      W['cv_ln_g'].reshape(DEPTH, 1, CV_WIDTH), W['cv_ln_b'].reshape(DEPTH, 1, CV_WIDTH))


def _merge2_kernel(y_ref, x_ref, w_ref, ng1_ref, g1_ref, ng2_ref, sc2_ref, sh2_ref, wr_ref, br_ref,
                   xo_ref, f_ref, lg_ref, wb_ref):
    @pl.when(pl.program_id(0) == 0)
    def _():
        wb_ref[...] = w_ref[...].astype(BF16)

    m = jnp.dot(y_ref[...], wb_ref[...], preferred_element_type=F32)
    xn = x_ref[...] + g1_ref[...] * (_rms(m) * ng1_ref[...])
    xo_ref[...] = xn
    f = _rms(xn) * ng2_ref[...] * (1.0 + sc2_ref[...]) + sh2_ref[...]
    f_ref[...] = f.astype(f_ref.dtype)
    lg_ref[...] = _dot3(f, wr_ref[...]) + br_ref[...]


def _merge2(y, x, W, ng3, mod3, layer, mod_row, *, bm=256):
    m, d = x.shape
    bm = min(bm, m)
    row = pl.BlockSpec((bm, d), lambda i: (i, 0))
    return pl.pallas_call(
        _merge2_kernel,
        grid=(m // bm,),
        in_specs=[row, row, _once((None, d, d), lambda i: (layer, 0, 0)),
                  _vec(layer * 4 + 1), _vec(mod_row * 6 + 2), _vec(layer * 4 + 2),
                  _vec(mod_row * 6 + 4), _vec(mod_row * 6 + 3),
                  pl.BlockSpec((None, d, N_EXPERTS), lambda i: (layer, 0, 0)),
                  pl.BlockSpec((None, 1, N_EXPERTS), lambda i: (layer, 0, 0))],
        out_specs=[row, row, pl.BlockSpec((bm, N_EXPERTS), lambda i: (i, 0))],
        out_shape=[jax.ShapeDtypeStruct((m, d), F32), jax.ShapeDtypeStruct((m, d), BF16),
                   jax.ShapeDtypeStruct((m, N_EXPERTS), F32)],
        scratch_shapes=[pltpu.VMEM((d, d), BF16)],
        compiler_params=_params("arbitrary"),
        name="out_proj_norm_router",
    )(y, x, W['w_out'], ng3, mod3, ng3, mod3, mod3, W['w_router'],
      W['b_router'].reshape(DEPTH, 1, N_EXPERTS))


def _moe_gu_kernel(be_ref, first_ref, x_ref, wg_ref, wu_ref, bg_ref, bu_ref, o_ref, wgb_ref, wub_ref):
    it = pl.program_id(1)

    @pl.when(first_ref[it] == 1)
    def _():
        wgb_ref[...] = wg_ref[...].astype(BF16)
        wub_ref[...] = wu_ref[...].astype(BF16)

    x = x_ref[...]
    gate = jnp.dot(x, wgb_ref[...], preferred_element_type=F32) + bg_ref[...]
    up = jnp.dot(x, wub_ref[...], preferred_element_type=F32) + bu_ref[...]
    gate = jnp.minimum(gate, SWIGLU_LIMIT)
    up = jnp.clip(up, -SWIGLU_LIMIT, SWIGLU_LIMIT)
    act = gate * jax.nn.sigmoid(SWIGLU_ALPHA * gate) * (up + 1.0)
    o_ref[...] = act.astype(o_ref.dtype)


def _moe_dn_kernel(be_ref, first_ref, a_ref, w_ref, b_ref, rw_ref, o_ref, wb_ref):
    it = pl.program_id(0)

    @pl.when(first_ref[it] == 1)
    def _():
        wb_ref[...] = w_ref[...].astype(BF16)

    y = jnp.dot(a_ref[...], wb_ref[...], preferred_element_type=F32) + b_ref[...]
    o_ref[...] = (y * rw_ref[...]).astype(o_ref.dtype)


def _moe_ffn(xs, row_w, block_e, first, w_gu, b_gu, w_dn, b_dn, layer, *, bn=512):
    cap, d = xs.shape
    n_items = cap // MOE_ROWS
    nj = D_EXPERT // bn
    act = pl.pallas_call(
        _moe_gu_kernel,
        grid_spec=pltpu.PrefetchScalarGridSpec(
            num_scalar_prefetch=2,
            grid=(nj, n_items),
            in_specs=[pl.BlockSpec((MOE_ROWS, d), lambda j, i, be, fi: (i, 0)),
                      pl.BlockSpec((None, None, d, bn), lambda j, i, be, fi: (layer, be[i], 0, j)),
                      pl.BlockSpec((None, None, d, bn), lambda j, i, be, fi: (layer, be[i], 0, j + nj)),
                      pl.BlockSpec((None, None, 1, bn), lambda j, i, be, fi: (layer, be[i], 0, j)),
                      pl.BlockSpec((None, None, 1, bn), lambda j, i, be, fi: (layer, be[i], 0, j + nj))],
            out_specs=pl.BlockSpec((MOE_ROWS, bn), lambda j, i, be, fi: (i, j)),
            scratch_shapes=[pltpu.VMEM((d, bn), BF16), pltpu.VMEM((d, bn), BF16)]),
        out_shape=jax.ShapeDtypeStruct((cap, D_EXPERT), BF16),
        compiler_params=_params("arbitrary", "arbitrary"),
        name="moe_gate_up",
    )(block_e, first, xs, w_gu, w_gu, b_gu.reshape(DEPTH, N_EXPERTS, 1, 2 * D_EXPERT),
      b_gu.reshape(DEPTH, N_EXPERTS, 1, 2 * D_EXPERT))
    return pl.pallas_call(
        _moe_dn_kernel,
        grid_spec=pltpu.PrefetchScalarGridSpec(
            num_scalar_prefetch=2,
            grid=(n_items,),
            in_specs=[pl.BlockSpec((MOE_ROWS, D_EXPERT), lambda i, be, fi: (i, 0)),
                      pl.BlockSpec((None, None, D_EXPERT, d), lambda i, be, fi: (layer, be[i], 0, 0)),
                      pl.BlockSpec((None, None, 1, d), lambda i, be, fi: (layer, be[i], 0, 0)),
                      pl.BlockSpec((MOE_ROWS, 1), lambda i, be, fi: (i, 0))],
            out_specs=pl.BlockSpec((MOE_ROWS, d), lambda i, be, fi: (i, 0)),
            scratch_shapes=[pltpu.VMEM((D_EXPERT, d), BF16)]),
        out_shape=jax.ShapeDtypeStruct((cap, d), F32),
        compiler_params=_params("arbitrary"),
        name="moe_down",
    )(block_e, first, act, w_dn, b_dn.reshape(DEPTH, N_EXPERTS, 1, d), row_w.reshape(cap, 1))


def _moe(h, logits, W, layer):
    n, d = h.shape
    top_val, top_idx = lax.top_k(logits, TOP_K)
    gates = jax.nn.softmax(top_val, axis=-1)
    flat_e = top_idx.reshape(-1)
    order = jnp.argsort(flat_e)
    sorted_e = flat_e[order]
    tok = (order // TOP_K).astype(jnp.int32)
    w_sorted = gates.reshape(-1)[order]
    counts = jnp.bincount(flat_e, length=N_EXPERTS)
    padded = (counts + MOE_ROWS - 1) // MOE_ROWS * MOE_ROWS
    pad_end = jnp.cumsum(padded)
    pad_start = pad_end - padded
    grp_start = jnp.cumsum(counts) - counts
    rank = jnp.arange(n * TOP_K) - grp_start[sorted_e]
    dest = pad_start[sorted_e] + rank
    n_items = -(-(n * TOP_K) // MOE_ROWS) + N_EXPERTS
    cap = n_items * MOE_ROWS
    row_tok = jnp.zeros((cap,), jnp.int32).at[dest].set(tok)
    row_w = jnp.zeros((cap,), F32).at[dest].set(w_sorted)
    block_e = jnp.minimum(jnp.searchsorted(pad_end, jnp.arange(n_items) * MOE_ROWS, side='right'),
                          N_EXPERTS - 1).astype(jnp.int32)
    first = jnp.concatenate([jnp.ones((1,), jnp.int32),
                             (block_e[1:] != block_e[:-1]).astype(jnp.int32)])
    xs = h[row_tok]
    ys = _moe_ffn(xs, row_w, block_e, first, W['w_gu'], W['b_gu'], W['w_dn'], W['b_dn'], layer)
    return jnp.zeros((n, d), F32).at[row_tok].add(ys)


def _mixer_tail(proj, a, W, layer):
    hy = _hyena_branch(proj, W, layer)
    cvraw = _dwconv(proj, OFF_CV, CV_WIDTH, W['cv_dw_w'], W['cv_dw_b'], layer, glu=True)
    return _merge1(proj, a, hy, cvraw, W, layer)


def _trunk_layer(x, ctx, c8, W, ng3, layer, need_ctx):
    s = x.shape[0]
    lc = ctx.shape[0]
    lam_init = 0.8 - 0.6 * math.exp(-0.3 * layer)
    mod3 = _ada(c8, W['w_ada'], W['b_ada'], layer).reshape(-1, 1, D_MODEL)
    h_lat = _norm_mod(x, ng3, mod3, layer * 4, 1, 0)
    h_ctx = _norm_mod(ctx, ng3, mod3, layer * 4, 6 + 1, 6 + 0)
    attn = functools.partial(_diff_attention, da_lambda=W['da_lambda'], da_subln_g=W['da_subln_g'],
                             layer=layer, lam_init=lam_init)
    proj = _mm(h_lat, W['w_in'], layer)
    tabs = _rope_tables(s)
    if need_ctx:
        proj_c = _mm(h_ctx, W['w_in'], layer)
        kt_all, v_all = _kv_prep(proj, proj_c, OFF_K // K_COLS, tabs)
    else:
        kv_c = _mm(h_ctx, W['w_in'], layer, col_off=OFF_K, n_cols=K_COLS + V_COLS)
        kt_all, v_all = _kv_prep(proj, kv_c, 0, tabs)
    a_lat = attn(_q_prep(proj, tabs, rope=True), kt_all, v_all)
    x, f_lat, lg_lat = _merge2(_mixer_tail(proj, a_lat, W, layer), x, W, ng3, mod3, layer, 0)
    if need_ctx:
        a_ctx = attn(_q_prep(proj_c, tabs, rope=False), kt_all[:, :lc], v_all[:lc])
        ctx, f_ctx, lg_ctx = _merge2(_mixer_tail(proj_c, a_ctx, W, layer), ctx, W, ng3, mod3, layer, 1)
        f = _moe(jnp.concatenate([f_ctx, f_lat], axis=0), jnp.concatenate([lg_ctx, lg_lat], axis=0), W, layer)
        ctx = _resid_norm(ctx, f, ng3, mod3, layer * 4 + 3, 6 + 5)
        x = _resid_norm(x, f, ng3, mod3, layer * 4 + 3, 5, row_off=lc)
    else:
        f = _moe(f_lat, lg_lat, W, layer)
        x = _resid_norm(x, f, ng3, mod3, layer * 4 + 3, 5)
    return x, ctx


def kernel(x, c, ctx, c_ctx, w_ada, b_ada, norm_g, w_in, b_gate, da_lambda, da_subln_g, w_da_out, hy_short_w, hy_short_b, hy_w1, hy_b1, hy_w2, hy_b2, hy_freq, hy_w3, hy_bias, w_hy_out, cv_dw_w, cv_dw_b, cv_ln_g, cv_ln_b, w_cv_out, w_out, w_router, b_router, w_gu, b_gu, w_dn, b_dn):
    W = dict(w_ada=w_ada, b_ada=b_ada, w_in=w_in, b_gate=b_gate, da_lambda=da_lambda, da_subln_g=da_subln_g,
             w_da_out=w_da_out, hy_short_w=hy_short_w, hy_short_b=hy_short_b, hy_w1=hy_w1, hy_b1=hy_b1,
             hy_w2=hy_w2, hy_b2=hy_b2, hy_freq=hy_freq, hy_w3=hy_w3, hy_bias=hy_bias, w_hy_out=w_hy_out,
             cv_dw_w=cv_dw_w, cv_dw_b=cv_dw_b, cv_ln_g=cv_ln_g, cv_ln_b=cv_ln_b, w_cv_out=w_cv_out,
             w_out=w_out, w_router=w_router, b_router=b_router, w_gu=w_gu, b_gu=b_gu, w_dn=w_dn, b_dn=b_dn)
    assert x.shape[0] == 1 and ctx.shape[0] == 1
    xl = x[0]
    cl = ctx[0]
    c8 = jnp.concatenate([c, c_ctx[None], jnp.zeros((6, D_MODEL), F32)], axis=0)
    ng3 = norm_g.reshape(DEPTH * 4, 1, D_MODEL)
    for layer in range(DEPTH):
        xl, cl = _trunk_layer(xl, cl, c8, W, ng3, layer, layer < DEPTH - 1)
    return xl[None]
```

```python
import functools
import math

import numpy as np
import jax
import jax.numpy as jnp
from jax import lax
from jax.experimental import pallas as pl
from jax.experimental.pallas import tpu as pltpu

F32 = jnp.float32
BF16 = jnp.bfloat16

D_MODEL = 2048
DEPTH = 2
GRID_W = 64
EPS = 1e-6

DA_HEADS = 8
DA_QK_DIM = 64
DA_V_DIM = 2 * DA_QK_DIM
DA_WIDTH = DA_HEADS * DA_V_DIM
DA_SCALE = DA_QK_DIM ** -0.5
LOG2E = math.log2(math.e)
ROPE_BASE = 10000.0

HY_WIDTH = 1024
HY_ORDER = 2
HY_SHORT = 3
HY_EMB = 33
HY_BANDS = (HY_EMB - 1) // 2
HY_FFN = 64
HY_MIN_DECAY = math.log(1e-2) / 1.5
HY_MAX_DECAY = math.log(1e-2) / 0.3

CV_WIDTH = 1024
CV_KERNEL = 31
N_BRANCH = 3

N_EXPERTS = 32
TOP_K = 4
D_EXPERT = 1024
SWIGLU_LIMIT = 7.0
SWIGLU_ALPHA = 1.702

Q_COLS = 2 * DA_HEADS * DA_QK_DIM
K_COLS = Q_COLS
V_COLS = DA_WIDTH
HY_COLS = (HY_ORDER + 1) * HY_WIDTH
CV_COLS = 2 * CV_WIDTH
GATE_COLS = N_BRANCH * D_MODEL
OFF_Q = 0
OFF_K = OFF_Q + Q_COLS
OFF_V = OFF_K + K_COLS
OFF_HY = OFF_V + V_COLS
OFF_CV = OFF_HY + HY_COLS
OFF_GATE = OFF_CV + CV_COLS
IN_COLS = OFF_GATE + GATE_COLS

LANES = 128
VMEM_LIMIT_BYTES = 56 * 1024 * 1024
MOE_ROWS = 256
FFT_R = 128
FFT_C = 128
CONV_HALO = 16


def _params(*sem):
    return pltpu.CompilerParams(dimension_semantics=sem, vmem_limit_bytes=VMEM_LIMIT_BYTES)


def _once(shape, index_map):
    return pl.BlockSpec(shape, index_map, pipeline_mode=pl.Buffered(1))


def _split_bf16(a):
    hi = a.astype(BF16)
    return hi, (a - hi.astype(F32)).astype(BF16)


def _dot3(a, b):
    ah, al = _split_bf16(a)
    bh, bl = _split_bf16(b)
    d = functools.partial(jnp.dot, preferred_element_type=F32)
    return d(ah, bh) + (d(ah, bl) + d(al, bh))


def _rms(x):
    return x * lax.rsqrt(jnp.mean(x * x, axis=-1, keepdims=True) + EPS)


def _mm_kernel(x_ref, w_ref, o_ref, wb_ref):
    @pl.when(pl.program_id(1) == 0)
    def _():
        wb_ref[...] = w_ref[...].astype(BF16)

    o_ref[...] = jnp.dot(x_ref[...].astype(BF16), wb_ref[...],
                         preferred_element_type=F32).astype(o_ref.dtype)


def _mm(x, w, layer, *, col_off=0, n_cols=None, out_dtype=F32, bm=1024, bn=1024):
    m, k = x.shape
    n = w.shape[2] if n_cols is None else n_cols
    bm = min(bm, m)
    bn = min(bn, n)
    assert m % bm == 0 and n % bn == 0 and col_off % bn == 0
    joff = col_off // bn
    return pl.pallas_call(
        _mm_kernel,
        grid=(n // bn, m // bm),
        in_specs=[pl.BlockSpec((bm, k), lambda j, i: (i, 0)),
                  pl.BlockSpec((None, k, bn), lambda j, i: (layer, 0, j + joff))],
        out_specs=pl.BlockSpec((bm, bn), lambda j, i: (i, j)),
        out_shape=jax.ShapeDtypeStruct((m, n), out_dtype),
        scratch_shapes=[pltpu.VMEM((k, bn), BF16)],
        compiler_params=_params("arbitrary", "arbitrary"),
        name="dense_mm",
    )(x, w)


def _ada_kernel(c_ref, w_ref, b_ref, o_ref):
    c = c_ref[...]
    o_ref[...] = _dot3(c * jax.nn.sigmoid(c), w_ref[...]) + b_ref[...]


def _ada(c8, w_ada, b_ada, layer, *, bn=1024):
    r, d = c8.shape
    n = w_ada.shape[2]
    return pl.pallas_call(
        _ada_kernel,
        grid=(n // bn,),
        in_specs=[pl.BlockSpec((r, d), lambda j: (0, 0)),
                  pl.BlockSpec((None, d, bn), lambda j: (layer, 0, j)),
                  pl.BlockSpec((None, 1, bn), lambda j: (layer, 0, j))],
        out_specs=pl.BlockSpec((r, bn), lambda j: (0, j)),
        out_shape=jax.ShapeDtypeStruct((r, n), F32),
        compiler_params=_params("arbitrary"),
        name="ada_mod",
    )(c8, w_ada, b_ada.reshape(DEPTH, 1, n))


def _norm_mod_kernel(x_ref, g_ref, sc_ref, sh_ref, o_ref):
    y = _rms(x_ref[...]) * g_ref[...]
    o_ref[...] = (y * (1.0 + sc_ref[...]) + sh_ref[...]).astype(o_ref.dtype)


def _vec(idx):
    return pl.BlockSpec((None, 1, D_MODEL), lambda *_: (idx, 0, 0))


def _norm_mod(x, ng3, mod3, ng_idx, sc_idx, sh_idx, *, bm=512):
    m, d = x.shape
    bm = min(bm, m)
    return pl.pallas_call(
        _norm_mod_kernel,
        grid=(m // bm,),
        in_specs=[pl.BlockSpec((bm, d), lambda i: (i, 0)), _vec(ng_idx), _vec(sc_idx), _vec(sh_idx)],
        out_specs=pl.BlockSpec((bm, d), lambda i: (i, 0)),
        out_shape=jax.ShapeDtypeStruct((m, d), BF16),
        compiler_params=_params("arbitrary"),
        name="norm_mod",
    )(x, ng3, mod3, mod3)


def _resid_kernel(x_ref, f_ref, ng_ref, gate_ref, o_ref):
    o_ref[...] = x_ref[...] + gate_ref[...] * (_rms(f_ref[...]) * ng_ref[...])


def _resid_norm(x, f, ng3, mod3, ng_idx, gate_idx, *, row_off=0, bm=256):
    m, d = x.shape
    bm = min(bm, m)
    assert row_off % bm == 0
    roff = row_off // bm
    return pl.pallas_call(
        _resid_kernel,
        grid=(m // bm,),
        in_specs=[pl.BlockSpec((bm, d), lambda i: (i, 0)), pl.BlockSpec((bm, d), lambda i: (i + roff, 0)),
                  _vec(ng_idx), _vec(gate_idx)],
        out_specs=pl.BlockSpec((bm, d), lambda i: (i, 0)),
        out_shape=jax.ShapeDtypeStruct((m, d), F32),
        compiler_params=_params("arbitrary"),
        name="resid_norm",
    )(x, f, ng3, mod3)


def _rope_tile(x, cos, sa, sb):
    q4 = DA_QK_DIM // 4
    return x * cos + pltpu.roll(x, LANES - q4, 1) * sa + pltpu.roll(x, q4, 1) * sb


def _q_prep_kernel(p_ref, cos_ref, sa_ref, sb_ref, o_ref, *, rope):
    for h in range(DA_HEADS):
        sl = slice(h * LANES, (h + 1) * LANES)
        x = p_ref[:, sl]
        if rope:
            x = _rope_tile(x, cos_ref[...], sa_ref[...], sb_ref[...])
        o_ref[:, sl] = (x * (DA_SCALE * LOG2E)).astype(BF16)


def _q_prep(proj, tabs, *, rope, bm=256):
    m = proj.shape[0]
    bm = min(bm, m)
    tab = pl.BlockSpec((bm, LANES), lambda i: (i, 0))
    return pl.pallas_call(
        functools.partial(_q_prep_kernel, rope=rope),
        grid=(m // bm,),
        in_specs=[pl.BlockSpec((bm, Q_COLS), lambda i: (i, OFF_Q // Q_COLS)), tab, tab, tab],
        out_specs=pl.BlockSpec((bm, Q_COLS), lambda i: (i, 0)),
        out_shape=jax.ShapeDtypeStruct((m, Q_COLS), BF16),
        compiler_params=_params("arbitrary"),
        name="q_prep",
    )(proj, *tabs)


def _kv_prep_kernel(kc_ref, vc_ref, k_ref, v_ref, cos_ref, sa_ref, sb_ref, kt_ref, vo_ref):
    i = pl.program_id(0)

    @pl.when(i == 0)
    def _():
        for h in range(DA_HEADS):
            sl = slice(h * LANES, (h + 1) * LANES)
            kt_ref[sl, :] = kc_ref[:, sl].T.astype(BF16)
        vo_ref[...] = vc_ref[...].astype(BF16)

    @pl.when(i > 0)
    def _():
        for h in range(DA_HEADS):
            sl = slice(h * LANES, (h + 1) * LANES)
            kt_ref[sl, :] = _rope_tile(k_ref[:, sl], cos_ref[...], sa_ref[...], sb_ref[...]).T.astype(BF16)
        vo_ref[...] = v_ref[...].astype(BF16)


def _kv_prep(proj, ctx_kv, ctx_k_blk, tabs):
    s = proj.shape[0]
    lc = ctx_kv.shape[0]
    assert s % lc == 0
    nb = s // lc + 1
    lat = lambda i: (jnp.maximum(i - 1, 0), 0)
    tab = pl.BlockSpec((lc, LANES), lat)
    return pl.pallas_call(
        _kv_prep_kernel,
        grid=(nb,),
        in_specs=[pl.BlockSpec((lc, K_COLS), lambda i: (0, ctx_k_blk)),
                  pl.BlockSpec((lc, V_COLS), lambda i: (0, ctx_k_blk + 1)),
                  pl.BlockSpec((lc, K_COLS), lambda i: (jnp.maximum(i - 1, 0), OFF_K // K_COLS)),
                  pl.BlockSpec((lc, V_COLS), lambda i: (jnp.maximum(i - 1, 0), OFF_V // V_COLS)),
                  tab, tab, tab],
        out_specs=[pl.BlockSpec((K_COLS, lc), lambda i: (0, i)),
                   pl.BlockSpec((lc, V_COLS), lambda i: (i, 0))],
        out_shape=[jax.ShapeDtypeStruct((K_COLS, s + lc), BF16),
                   jax.ShapeDtypeStruct((s + lc, V_COLS), BF16)],
        compiler_params=_params("arbitrary"),
        name="kv_prep",
    )(ctx_kv, ctx_kv, proj, proj, *tabs)


def _rope_tables(n_lat):
    rows = n_lat // GRID_W
    row = jnp.repeat(jnp.arange(rows, dtype=F32), GRID_W)
    col = jnp.tile(jnp.arange(GRID_W, dtype=F32), rows)
    half = DA_QK_DIM // 2
    inv = ROPE_BASE ** (-jnp.arange(0, half, 2, dtype=F32) / half)
    ar = row[:, None] * inv
    ac = col[:, None] * inv
    zr = jnp.zeros_like(ar)
    cos = jnp.concatenate([jnp.cos(ar), jnp.cos(ar), jnp.cos(ac), jnp.cos(ac)], axis=-1)
    sa = jnp.concatenate([-jnp.sin(ar), zr, -jnp.sin(ac), zr], axis=-1)
    sb = jnp.concatenate([zr, jnp.sin(ar), zr, jnp.sin(ac)], axis=-1)
    return tuple(jnp.tile(t, (1, LANES // DA_QK_DIM)) for t in (cos, sa, sb))


def _attn_kernel(lp_ref, g_ref, q_ref, kt_ref, v_ref, o_ref, *, bk, lam_init):
    bq = q_ref.shape[0]
    nk = kt_ref.shape[1] // bk
    q = q_ref[...]
    lane = lax.broadcasted_iota(jnp.int32, q.shape, 1)
    zero = jnp.zeros_like(q)
    q2 = jnp.concatenate([jnp.where(lane < DA_QK_DIM, q, zero),
                          jnp.where(lane >= DA_QK_DIM, q, zero)], axis=0)

    def body(i, carry):
        m, l, acc = carry
        off = pl.multiple_of(i * bk, bk)
        s = jnp.dot(q2, kt_ref[:, pl.ds(off, bk)], preferred_element_type=F32)
        m_new = jnp.maximum(m, jnp.max(s, axis=-1, keepdims=True))
        alpha = jnp.exp2(m - m_new)
        p = jnp.exp2(s - m_new)
        l = alpha * l + jnp.sum(p, axis=-1, keepdims=True)
        acc = alpha * acc + jnp.dot(p.astype(BF16), v_ref[pl.ds(off, bk), :],
                                    preferred_element_type=F32)
        return m_new, l, acc

    m0 = jnp.full((2 * bq, 1), -jnp.inf, F32)
    l0 = jnp.zeros((2 * bq, 1), F32)
    a0 = jnp.zeros((2 * bq, DA_V_DIM), F32)
    _, l, acc = lax.fori_loop(0, nk, body, (m0, l0, a0))
    o = acc / l
    lp = lp_ref[...].astype(F32)
    lam = (jnp.exp(jnp.sum(lp[0:1] * lp[1:2], axis=-1, keepdims=True))
           - jnp.exp(jnp.sum(lp[2:3] * lp[3:4], axis=-1, keepdims=True)) + lam_init)
    o = o[:bq] - lam * o[bq:]
    y = _rms(o) * g_ref[...].astype(F32) * (1.0 - lam_init)
    o_ref[...] = y.astype(o_ref.dtype)


def _diff_attention(q, kt, v, da_lambda, da_subln_g, layer, lam_init, *, bq=256, bk=None):
    sq = q.shape[0]
    sk = v.shape[0]
    bq = min(bq, sq)
    if bk is None:
        bk = 768 if sk % 768 == 0 else sk
    hw = 2 * DA_QK_DIM
    return pl.pallas_call(
        functools.partial(_attn_kernel, bk=bk, lam_init=lam_init),
        grid=(DA_HEADS, sq // bq),
        in_specs=[pl.BlockSpec((None, 4, DA_QK_DIM), lambda h, i: (layer, 0, 0)),
                  pl.BlockSpec((None, 1, DA_V_DIM), lambda h, i: (layer, 0, 0)),
                  pl.BlockSpec((bq, hw), lambda h, i: (i, h)),
                  pl.BlockSpec((hw, sk), lambda h, i: (h, 0)),
                  pl.BlockSpec((sk, DA_V_DIM), lambda h, i: (0, h))],
        out_specs=pl.BlockSpec((bq, DA_V_DIM), lambda h, i: (i, h)),
        out_shape=jax.ShapeDtypeStruct((sq, DA_WIDTH), BF16),
        compiler_params=_params("arbitrary", "arbitrary"),
        name="diff_attention",
    )(da_lambda, da_subln_g.reshape(DEPTH, 1, DA_V_DIM), q, kt, v)


def _dwconv_kernel(*refs, ntaps, glu, n_row_blocks):
    if glu:
        cur, prv, nxt, gcur, gprv, gnxt, w_ref, b_ref, o_ref, pad_ref = refs
    else:
        cur, prv, nxt, w_ref, b_ref, o_ref, pad_ref = refs
        gcur = gprv = gnxt = None
    i = pl.program_id(1)
    t, cw = cur.shape
    h = CONV_HALO
    half = (ntaps - 1) // 2

    def val(a, g):
        return a[...] * jax.nn.sigmoid(g[...]) if glu else a[...]

    pad_ref[pl.ds(h, t), :] = val(cur, gcur)
    pad_ref[pl.ds(0, h), :] = jnp.where(i > 0, val(prv, gprv), 0.0)
    pad_ref[pl.ds(h + t, h), :] = jnp.where(i < n_row_blocks - 1, val(nxt, gnxt), 0.0)
    rc = 64
    for r in range(0, t, rc):
        for c in range(0, cw, LANES):
            acc = jnp.broadcast_to(b_ref[:, c:c + LANES], (rc, LANES))
            for j in range(ntaps):
                acc = acc + w_ref[j:j + 1, c:c + LANES] * pad_ref[pl.ds(r + h - half + j, rc), c:c + LANES]
            o_ref[pl.ds(r, rc), c:c + LANES] = acc


def _dwconv(u, col_off, n_ch, w, b, layer, *, glu=False, t=512, cw=256):
    s = u.shape[0]
    ntaps = w.shape[1]
    t = min(t, s)
    assert s % t == 0 and n_ch % cw == 0 and col_off % cw == 0 and t % CONV_HALO == 0
    nrb = s // t
    hb = t // CONV_HALO
    last_h = s // CONV_HALO - 1
    coff = col_off // cw
    goff = (col_off + n_ch) // cw

    def specs(off):
        return [pl.BlockSpec((t, cw), lambda c, i: (i, c + off)),
                pl.BlockSpec((CONV_HALO, cw), lambda c, i: (jnp.maximum(i * hb - 1, 0), c + off)),
                pl.BlockSpec((CONV_HALO, cw), lambda c, i: (jnp.minimum((i + 1) * hb, last_h), c + off))]

    in_specs = specs(coff) + (specs(goff) if glu else [])
    args = [u] * (6 if glu else 3)
    in_specs += [pl.BlockSpec((None, ntaps, cw), lambda c, i: (layer, 0, c)),
                 pl.BlockSpec((None, 1, cw), lambda c, i: (layer, 0, c))]
    return pl.pallas_call(
        functools.partial(_dwconv_kernel, ntaps=ntaps, glu=glu, n_row_blocks=nrb),
        grid=(n_ch // cw, nrb),
        in_specs=in_specs,
        out_specs=pl.BlockSpec((t, cw), lambda c, i: (i, c)),
        out_shape=jax.ShapeDtypeStruct((s, n_ch), F32),
        scratch_shapes=[pltpu.VMEM((t + 2 * CONV_HALO, cw), F32)],
        compiler_params=_params("arbitrary", "arbitrary"),
        name="dwconv",
    )(*args, w, b.reshape(DEPTH, 1, n_ch))


def _hy_filter_kernel(z_ref, w1_ref, b1_ref, w2_ref, b2_ref, fr_ref, w3_ref, dl_ref, h_ref, s_ref):
    z = z_ref[...]
    fr = fr_ref[...]
    a = jnp.sin(fr * (_dot3(z, w1_ref[...]) + b1_ref[...]))
    a = jnp.sin(fr * (_dot3(a, w2_ref[...]) + b2_ref[...]))
    hh = _dot3(a, w3_ref[...]) * jnp.exp(-z[:, 0:1] * dl_ref[...])
    h_ref[...] = hh

    @pl.when(pl.program_id(0) == 0)
    def _():
        s_ref[...] = jnp.zeros_like(s_ref)

    s_ref[...] += jnp.sum(jnp.abs(hh), axis=0, keepdims=True)


def _hy_filters(n, hy_w1, hy_b1, hy_w2, hy_b2, hy_freq, hy_w3, layer, *, t=512):
    t = min(t, n)
    tt = jnp.linspace(0.0, 1.0, n, dtype=F32)[:, None]
    w = 2.0 * math.pi * jnp.arange(n, dtype=F32)[:, None] / n
    bands = jnp.linspace(1e-4, HY_BANDS - 1, HY_BANDS, dtype=F32)[None, :]
    z = jnp.concatenate([tt, jnp.cos(bands * w), -jnp.sin(bands * w),
                         jnp.zeros((n, LANES - HY_EMB), F32)], axis=-1)
    w1 = jnp.pad(hy_w1, ((0, 0), (0, LANES - HY_EMB), (0, 0)))
    deltas = jnp.abs(jnp.linspace(HY_MIN_DECAY, HY_MAX_DECAY, HY_WIDTH, dtype=F32))
    nf = HY_ORDER * 2 * HY_WIDTH
    dl = jnp.tile(deltas, HY_ORDER * 2)[None, :]
    row = lambda a: a.reshape(DEPTH, 1, HY_FFN)
    small = pl.BlockSpec((None, 1, HY_FFN), lambda i: (layer, 0, 0))
    return pl.pallas_call(
        _hy_filter_kernel,
        grid=(n // t,),
        in_specs=[pl.BlockSpec((t, LANES), lambda i: (i, 0)),
                  pl.BlockSpec((None, LANES, HY_FFN), lambda i: (layer, 0, 0)), small,
                  pl.BlockSpec((None, HY_FFN, HY_FFN), lambda i: (layer, 0, 0)), small, small,
                  pl.BlockSpec((None, HY_FFN, nf), lambda i: (layer, 0, 0)),
                  pl.BlockSpec((1, nf), lambda i: (0, 0))],
        out_specs=[pl.BlockSpec((t, nf), lambda i: (i, 0)), pl.BlockSpec((1, nf), lambda i: (0, 0))],
        out_shape=[jax.ShapeDtypeStruct((n, nf), F32), jax.ShapeDtypeStruct((1, nf), F32)],
        compiler_params=_params("arbitrary"),
        name="hy_filters",
    )(z, w1, row(hy_b1), hy_w2, row(hy_b2), row(hy_freq), hy_w3, dl)


def _angles(num, den):
    return 2.0 * np.pi * (np.asarray(num, np.int64) % den).astype(np.float64) / den


@functools.lru_cache(maxsize=None)
def _fft_tables():
    r = FFT_R
    n_fft = r * r
    hr = r // 2
    s2 = np.arange(r)[:, None, None]
    f1 = np.arange(r)[None, :, None]
    s1 = np.arange(hr)[None, None, :]
    th = _angles(f1 * (r * s1 + s2), n_fft)
    fwd = np.concatenate([np.cos(th), -np.sin(th)], axis=1)
    thb = _angles(f1 * (r * (r - 1 - s1) + s2), n_fft)
    bwd = np.concatenate([np.cos(thb), -np.sin(thb)], axis=1)
    fwd = np.concatenate([fwd, bwd], axis=2)
    t2 = _angles(np.arange(r)[:, None] * np.arange(r)[None, :], r)
    c, s = np.cos(t2), np.sin(t2)
    m2 = np.block([[c, s], [-s, c]])
    m3 = np.block([[c, -s], [s, c]])
    tt2 = np.arange(r)[:, None, None]
    tt1 = np.arange(hr)[None, :, None]
    ff1 = np.arange(r)[None, None, :]
    ph = _angles(ff1 * (r * tt1 + tt2), n_fft)
    g4 = np.concatenate([np.cos(ph), -np.sin(ph)], axis=2) / n_fft
    return tuple(jnp.asarray(a, dtype=BF16) for a in (fwd, m2, m3, g4))


def _store_stage1(a_ref, s2, out):
    r = FFT_R
    a_ref[pl.ds(s2, r, stride=2 * r), :] = out[:r]
    a_ref[pl.ds(s2 + r, r, stride=2 * r), :] = out[r:]


def _hy_spec_kernel(hf_ref, hb_ref, inv_ref, fwd_ref, m2_ref, k_ref, a_ref):
    r = FFT_R
    hr = r // 2

    def stage1(s2, c):
        xf = hf_ref[pl.ds(s2, hr, stride=r), :]
        xb = hb_ref[pl.ds(r - 1 - s2, hr, stride=r), :]
        xs = jnp.concatenate([xf, xb], axis=0).astype(BF16)
        _store_stage1(a_ref, s2, jnp.dot(fwd_ref[s2], xs, preferred_element_type=F32))
        return c

    lax.fori_loop(0, r, stage1, 0, unroll=4)

    def stage2(f1, c):
        r0 = pl.multiple_of(f1 * 2 * r, 2 * r)
        z = jnp.dot(m2_ref[...], a_ref[pl.ds(r0, 2 * r), :].astype(BF16), preferred_element_type=F32)
        k_ref[f1] = (z * inv_ref[...]).astype(k_ref.dtype)
        return c

    lax.fori_loop(0, r, stage2, 0, unroll=4)


def _hy_spectra(h, inv_norm, tables):
    n = h.shape[0]
    r = FFT_R
    assert 2 * n == r * r
    fwd, m2, _, _ = tables
    nc = HY_WIDTH // FFT_C
    return pl.pallas_call(
        _hy_spec_kernel,
        grid=(HY_ORDER, nc),
        in_specs=[_once((n, FFT_C), lambda o, c: (0, o * 2 * nc + c)),
                  _once((n, FFT_C), lambda o, c: (0, o * 2 * nc + nc + c)),
                  pl.BlockSpec((1, FFT_C), lambda o, c: (0, o * nc + c)),
                  _once((r, 2 * r, r), lambda o, c: (0, 0, 0)),
                  _once((2 * r, 2 * r), lambda o, c: (0, 0))],
        out_specs=pl.BlockSpec((None, None, r, 2 * r, FFT_C), lambda o, c: (o, c, 0, 0, 0)),
        out_shape=jax.ShapeDtypeStruct((HY_ORDER, nc, r, 2 * r, FFT_C), BF16),
        scratch_shapes=[pltpu.VMEM((2 * r * r, FFT_C), F32)],
        compiler_params=_params("arbitrary", "arbitrary"),
        name="hy_spectra",
    )(h, h, inv_norm, fwd, m2)


def _hy_conv_kernel(v_ref, x_ref, k_ref, bias_ref, fwd_ref, m2_ref, m3_ref, g4_ref, o_ref, a_ref):
    r = FFT_R
    hr = r // 2

    def stage1(s2, c):
        xs = v_ref[pl.ds(s2, hr, stride=r), :].astype(BF16)
        _store_stage1(a_ref, s2, jnp.dot(fwd_ref[s2][:, :hr], xs, preferred_element_type=F32))
        return c

    lax.fori_loop(0, r, stage1, 0, unroll=4)

    def stage23(f1, c):
        r0 = pl.multiple_of(f1 * 2 * r, 2 * r)
        z = jnp.dot(m2_ref[...], a_ref[pl.ds(r0, 2 * r), :].astype(BF16), preferred_element_type=F32)
        kk = k_ref[f1].astype(F32)
        zr, zi, kr, ki = z[:r], z[r:], kk[:r], kk[r:]
        y = jnp.concatenate([zr * kr - zi * ki, zr * ki + zi * kr], axis=0).astype(BF16)
        a_ref[pl.ds(r0, 2 * r), :] = jnp.dot(m3_ref[...], y, preferred_element_type=F32)
        return c

    lax.fori_loop(0, r, stage23, 0, unroll=4)

    def stage4(t2, c):
        b = jnp.concatenate([a_ref[pl.ds(t2, r, stride=2 * r), :],
                             a_ref[pl.ds(t2 + r, r, stride=2 * r), :]], axis=0).astype(BF16)
        y = jnp.dot(g4_ref[t2], b, preferred_element_type=F32)
        vs = v_ref[pl.ds(t2, hr, stride=r), :]
        xs = x_ref[pl.ds(t2, hr, stride=r), :]
        o_ref[pl.ds(t2, hr, stride=r), :] = (xs * (y + vs * bias_ref[...])).astype(o_ref.dtype)
        return c

    lax.fori_loop(0, r, stage4, 0, unroll=4)


def _hy_conv(zin, zin_off, xmul, xmul_off, spectra, order, hy_bias, layer, tables, out_dtype):
    n = zin.shape[0]
    r = FFT_R
    assert 2 * n == r * r and zin_off % FFT_C == 0 and xmul_off % FFT_C == 0
    fwd, m2, m3, g4 = tables
    nc = HY_WIDTH // FFT_C
    hr = r // 2
    zo = zin_off // FFT_C
    xo = xmul_off // FFT_C
    return pl.pallas_call(
        _hy_conv_kernel,
        grid=(nc,),
        in_specs=[_once((n, FFT_C), lambda c: (0, c + zo)),
                  _once((n, FFT_C), lambda c: (0, c + xo)),
                  _once((None, None, r, 2 * r, FFT_C), lambda c: (order, c, 0, 0, 0)),
                  pl.BlockSpec((None, 1, FFT_C), lambda c: (layer * HY_ORDER + order, 0, c)),
                  _once((r, 2 * r, r), lambda c: (0, 0, 0)),
                  _once((2 * r, 2 * r), lambda c: (0, 0)),
                  _once((2 * r, 2 * r), lambda c: (0, 0)),
                  _once((r, hr, 2 * r), lambda c: (0, 0, 0))],
        out_specs=pl.BlockSpec((n, FFT_C), lambda c: (0, c)),
        out_shape=jax.ShapeDtypeStruct((n, HY_WIDTH), out_dtype),
        scratch_shapes=[pltpu.VMEM((2 * r * r, FFT_C), F32)],
        compiler_params=_params("arbitrary"),
        name="hy_conv",
    )(zin, xmul, spectra, hy_bias.reshape(DEPTH * HY_ORDER, 1, HY_WIDTH), fwd, m2, m3, g4)


@functools.lru_cache(maxsize=None)
def _dft_tables(n):
    n_fft = 2 * n
    f = np.arange(n_fft)[:, None]
    s = np.arange(n)[None, :]
    th = _angles(f * s, n_fft)
    fz = np.concatenate([np.cos(th), -np.sin(th)], axis=0)
    thb = _angles(f * (n_fft - 1 - s), n_fft)
    fb = np.concatenate([np.cos(thb), -np.sin(thb)], axis=0)
    ph = _angles(np.arange(n)[:, None] * np.arange(n_fft)[None, :], n_fft)
    g = np.concatenate([np.cos(ph), -np.sin(ph)], axis=1) / n_fft
    return tuple(jnp.asarray(a, dtype=BF16) for a in (fz, fb, g))


def _hy_conv_small_kernel(v_ref, x_ref, hf_ref, hb_ref, inv_ref, bias_ref, fz_ref, fb_ref, g_ref, o_ref):
    d = functools.partial(jnp.dot, preferred_element_type=F32)
    v = v_ref[...]
    nf = fz_ref.shape[0] // 2
    z = d(fz_ref[...], v.astype(BF16))
    k = (d(fz_ref[...], hf_ref[...].astype(BF16)) + d(fb_ref[...], hb_ref[...].astype(BF16))) * inv_ref[...]
    zr, zi, kr, ki = z[:nf], z[nf:], k[:nf], k[nf:]
    y = jnp.concatenate([zr * kr - zi * ki, zr * ki + zi * kr], axis=0).astype(BF16)
    o_ref[...] = (x_ref[...] * (d(g_ref[...], y) + v * bias_ref[...])).astype(o_ref.dtype)


def _hy_conv_small(zin, zin_off, xmul, xmul_off, h, inv_norm, order, hy_bias, layer, out_dtype, *, cw=256):
    n = zin.shape[0]
    fz, fb, g = _dft_tables(n)
    nc = HY_WIDTH // cw
    zo = zin_off // cw
    xo = xmul_off // cw
    full = lambda a: pl.BlockSpec(a.shape, lambda c: (0, 0))
    return pl.pallas_call(
        _hy_conv_small_kernel,
        grid=(nc,),
        in_specs=[pl.BlockSpec((n, cw), lambda c: (0, c + zo)),
                  pl.BlockSpec((n, cw), lambda c: (0, c + xo)),
                  pl.BlockSpec((n, cw), lambda c: (0, order * 2 * nc + c)),
                  pl.BlockSpec((n, cw), lambda c: (0, order * 2 * nc + nc + c)),
                  pl.BlockSpec((1, cw), lambda c: (0, order * nc + c)),
                  pl.BlockSpec((None, 1, cw), lambda c: (layer * HY_ORDER + order, 0, c)),
                  full(fz), full(fb), full(g)],
        out_specs=pl.BlockSpec((n, cw), lambda c: (0, c)),
        out_shape=jax.ShapeDtypeStruct((n, HY_WIDTH), out_dtype),
        compiler_params=_params("arbitrary"),
        name="hy_conv_small",
    )(zin, xmul, h, h, inv_norm, hy_bias.reshape(DEPTH * HY_ORDER, 1, HY_WIDTH), fz, fb, g)


def _hyena_branch(proj, W, layer):
    n = proj.shape[0]
    u = _dwconv(proj, OFF_HY, HY_COLS, W['hy_short_w'], W['hy_short_b'], layer)
    h, hsum = _hy_filters(n, W['hy_w1'], W['hy_b1'], W['hy_w2'], W['hy_b2'], W['hy_freq'], W['hy_w3'], layer)
    inv_norm = (1.0 / jnp.sum(hsum.reshape(HY_ORDER, 2, HY_WIDTH), axis=1)).reshape(1, HY_ORDER * HY_WIDTH)
    if 2 * n == FFT_R * FFT_R:
        tables = _fft_tables()
        spectra = _hy_spectra(h, inv_norm, tables)
        z1 = _hy_conv(u, 2 * HY_WIDTH, u, 0, spectra, 0, W['hy_bias'], layer, tables, F32)
        return _hy_conv(z1, 0, u, HY_WIDTH, spectra, 1, W['hy_bias'], layer, tables, F32)
    z1 = _hy_conv_small(u, 2 * HY_WIDTH, u, 0, h, inv_norm, 0, W['hy_bias'], layer, F32)
    return _hy_conv_small(z1, 0, u, HY_WIDTH, h, inv_norm, 1, W['hy_bias'], layer, F32)


def _merge1_kernel(a_ref, hy_ref, cv_ref, g0_ref, g1_ref, g2_ref, bg0_ref, bg1_ref, bg2_ref,
                   wa_ref, wh_ref, wc_ref, lng_ref, lnb_ref, o_ref, wab_ref, whb_ref, wcb_ref):
    @pl.when(pl.program_id(1) == 0)
    def _():
        wab_ref[...] = wa_ref[...].astype(BF16)
        whb_ref[...] = wh_ref[...].astype(BF16)
        wcb_ref[...] = wc_ref[...].astype(BF16)

    d = functools.partial(jnp.dot, preferred_element_type=F32)
    cv = cv_ref[...]
    mu = jnp.mean(cv, axis=-1, keepdims=True)
    var = jnp.mean(jnp.square(cv - mu), axis=-1, keepdims=True)
    cv = (cv - mu) * lax.rsqrt(var + EPS) * lng_ref[...] + lnb_ref[...]
    cv = cv * jax.nn.sigmoid(cv)
    y = (jax.nn.sigmoid(g0_ref[...] + bg0_ref[...]) * d(a_ref[...], wab_ref[...])
         + jax.nn.sigmoid(g1_ref[...] + bg1_ref[...]) * d(hy_ref[...].astype(BF16), whb_ref[...])
         + jax.nn.sigmoid(g2_ref[...] + bg2_ref[...]) * d(cv.astype(BF16), wcb_ref[...]))
    o_ref[...] = y.astype(o_ref.dtype)


def _merge1(proj, a, hy, cvraw, W, layer, *, bm=512, bn=512):
    m = proj.shape[0]
    bm = min(bm, m)
    d = D_MODEL
    nj = d // bn
    gate = lambda b: pl.BlockSpec((bm, bn), lambda j, i: (i, (OFF_GATE + b * d) // bn + j))
    bias = lambda b: pl.BlockSpec((None, 1, bn), lambda j, i: (layer, 0, b * nj + j))
    wspec = lambda k: pl.BlockSpec((None, k, bn), lambda j, i: (layer, 0, j))
    act = lambda k: pl.BlockSpec((bm, k), lambda j, i: (i, 0))
    lnv = pl.BlockSpec((None, 1, CV_WIDTH), lambda j, i: (layer, 0, 0))
    return pl.pallas_call(
        _merge1_kernel,
        grid=(nj, m // bm),
        in_specs=[act(DA_WIDTH), act(HY_WIDTH), act(CV_WIDTH), gate(0), gate(1), gate(2),
                  bias(0), bias(1), bias(2), wspec(DA_WIDTH), wspec(HY_WIDTH), wspec(CV_WIDTH), lnv, lnv],
        out_specs=pl.BlockSpec((bm, bn), lambda j, i: (i, j)),
        out_shape=jax.ShapeDtypeStruct((m, d), BF16),
        scratch_shapes=[pltpu.VMEM((DA_WIDTH, bn), BF16), pltpu.VMEM((HY_WIDTH, bn), BF16),
                        pltpu.VMEM((CV_WIDTH, bn), BF16)],
        compiler_params=_params("arbitrary", "arbitrary"),
        name="merge_branches",
    )(a, hy, cvraw, proj, proj, proj, *([W['b_gate'].reshape(DEPTH, 1, GATE_COLS)] * 3),
      W['w_da_out'], W['w_hy_out'], W['w_cv_out'],
      W['cv_ln_g'].reshape(DEPTH, 1, CV_WIDTH), W['cv_ln_b'].reshape(DEPTH, 1, CV_WIDTH))


def _merge2_kernel(y_ref, x_ref, w_ref, ng1_ref, g1_ref, ng2_ref, sc2_ref, sh2_ref, wr_ref, br_ref,
                   xo_ref, f_ref, lg_ref, wb_ref):
    @pl.when(pl.program_id(0) == 0)
    def _():
        wb_ref[...] = w_ref[...].astype(BF16)

    m = jnp.dot(y_ref[...], wb_ref[...], preferred_element_type=F32)
    xn = x_ref[...] + g1_ref[...] * (_rms(m) * ng1_ref[...])
    xo_ref[...] = xn
    f = _rms(xn) * ng2_ref[...] * (1.0 + sc2_ref[...]) + sh2_ref[...]
    f_ref[...] = f.astype(f_ref.dtype)
    lg_ref[...] = _dot3(f, wr_ref[...]) + br_ref[...]


def _merge2(y, x, W, ng3, mod3, layer, mod_row, *, bm=256):
    m, d = x.shape
    bm = min(bm, m)
    row = pl.BlockSpec((bm, d), lambda i: (i, 0))
    return pl.pallas_call(
        _merge2_kernel,
        grid=(m // bm,),
        in_specs=[row, row, _once((None, d, d), lambda i: (layer, 0, 0)),
                  _vec(layer * 4 + 1), _vec(mod_row * 6 + 2), _vec(layer * 4 + 2),
                  _vec(mod_row * 6 + 4), _vec(mod_row * 6 + 3),
                  pl.BlockSpec((None, d, N_EXPERTS), lambda i: (layer, 0, 0)),
                  pl.BlockSpec((None, 1, N_EXPERTS), lambda i: (layer, 0, 0))],
        out_specs=[row, row, pl.BlockSpec((bm, N_EXPERTS), lambda i: (i, 0))],
        out_shape=[jax.ShapeDtypeStruct((m, d), F32), jax.ShapeDtypeStruct((m, d), BF16),
                   jax.ShapeDtypeStruct((m, N_EXPERTS), F32)],
        scratch_shapes=[pltpu.VMEM((d, d), BF16)],
        compiler_params=_params("arbitrary"),
        name="out_proj_norm_router",
    )(y, x, W['w_out'], ng3, mod3, ng3, mod3, mod3, W['w_router'],
      W['b_router'].reshape(DEPTH, 1, N_EXPERTS))


def _moe_gu_kernel(be_ref, first_ref, x_ref, wg_ref, wu_ref, bg_ref, bu_ref, o_ref, wgb_ref, wub_ref):
    it = pl.program_id(1)

    @pl.when(first_ref[it] == 1)
    def _():
        wgb_ref[...] = wg_ref[...].astype(BF16)
        wub_ref[...] = wu_ref[...].astype(BF16)

    x = x_ref[...]
    gate = jnp.dot(x, wgb_ref[...], preferred_element_type=F32) + bg_ref[...]
    up = jnp.dot(x, wub_ref[...], preferred_element_type=F32) + bu_ref[...]
    gate = jnp.minimum(gate, SWIGLU_LIMIT)
    up = jnp.clip(up, -SWIGLU_LIMIT, SWIGLU_LIMIT)
    act = gate * jax.nn.sigmoid(SWIGLU_ALPHA * gate) * (up + 1.0)
    o_ref[...] = act.astype(o_ref.dtype)


def _moe_dn_kernel(be_ref, first_ref, a_ref, w_ref, b_ref, rw_ref, o_ref, wb_ref):
    it = pl.program_id(0)

    @pl.when(first_ref[it] == 1)
    def _():
        wb_ref[...] = w_ref[...].astype(BF16)

    y = jnp.dot(a_ref[...], wb_ref[...], preferred_element_type=F32) + b_ref[...]
    o_ref[...] = (y * rw_ref[...]).astype(o_ref.dtype)


def _moe_ffn(xs, row_w, block_e, first, w_gu, b_gu, w_dn, b_dn, layer, *, bn=512):
    cap, d = xs.shape
    n_items = cap // MOE_ROWS
    nj = D_EXPERT // bn
    act = pl.pallas_call(
        _moe_gu_kernel,
        grid_spec=pltpu.PrefetchScalarGridSpec(
            num_scalar_prefetch=2,
            grid=(nj, n_items),
            in_specs=[pl.BlockSpec((MOE_ROWS, d), lambda j, i, be, fi: (i, 0)),
                      pl.BlockSpec((None, None, d, bn), lambda j, i, be, fi: (layer, be[i], 0, j)),
                      pl.BlockSpec((None, None, d, bn), lambda j, i, be, fi: (layer, be[i], 0, j + nj)),
                      pl.BlockSpec((None, None, 1, bn), lambda j, i, be, fi: (layer, be[i], 0, j)),
                      pl.BlockSpec((None, None, 1, bn), lambda j, i, be, fi: (layer, be[i], 0, j + nj))],
            out_specs=pl.BlockSpec((MOE_ROWS, bn), lambda j, i, be, fi: (i, j)),
            scratch_shapes=[pltpu.VMEM((d, bn), BF16), pltpu.VMEM((d, bn), BF16)]),
        out_shape=jax.ShapeDtypeStruct((cap, D_EXPERT), BF16),
        compiler_params=_params("arbitrary", "arbitrary"),
        name="moe_gate_up",
    )(block_e, first, xs, w_gu, w_gu, b_gu.reshape(DEPTH, N_EXPERTS, 1, 2 * D_EXPERT),
      b_gu.reshape(DEPTH, N_EXPERTS, 1, 2 * D_EXPERT))
    return pl.pallas_call(
        _moe_dn_kernel,
        grid_spec=pltpu.PrefetchScalarGridSpec(
            num_scalar_prefetch=2,
            grid=(n_items,),
            in_specs=[pl.BlockSpec((MOE_ROWS, D_EXPERT), lambda i, be, fi: (i, 0)),
                      pl.BlockSpec((None, None, D_EXPERT, d), lambda i, be, fi: (layer, be[i], 0, 0)),
                      pl.BlockSpec((None, None, 1, d), lambda i, be, fi: (layer, be[i], 0, 0)),
                      pl.BlockSpec((MOE_ROWS, 1), lambda i, be, fi: (i, 0))],
            out_specs=pl.BlockSpec((MOE_ROWS, d), lambda i, be, fi: (i, 0)),
            scratch_shapes=[pltpu.VMEM((D_EXPERT, d), BF16)]),
        out_shape=jax.ShapeDtypeStruct((cap, d), F32),
        compiler_params=_params("arbitrary"),
        name="moe_down",
    )(block_e, first, act, w_dn, b_dn.reshape(DEPTH, N_EXPERTS, 1, d), row_w.reshape(cap, 1))


def _gather_kernel(tok_ref, h_ref, o_ref, sem):
    def row_copy(r, tok):
        return pltpu.make_async_copy(h_ref.at[tok], o_ref.at[r], sem)

    def issue(r, c):
        row_copy(r, tok_ref[0, r]).start()
        return c

    lax.fori_loop(0, MOE_ROWS, issue, 0)

    def drain(r, c):
        row_copy(r, 0).wait()
        return c

    lax.fori_loop(0, MOE_ROWS, drain, 0)


def _gather_rows(h, row_tok):
    n, d = h.shape
    cap = row_tok.shape[0]
    n_items = cap // MOE_ROWS
    sub = d // LANES
    out = pl.pallas_call(
        _gather_kernel,
        grid=(n_items,),
        in_specs=[pl.BlockSpec((None, 1, MOE_ROWS), lambda i: (i, 0, 0), memory_space=pltpu.SMEM),
                  pl.BlockSpec(memory_space=pl.ANY)],
        out_specs=pl.BlockSpec((MOE_ROWS, sub, LANES), lambda i: (i, 0, 0)),
        out_shape=jax.ShapeDtypeStruct((cap, sub, LANES), h.dtype),
        scratch_shapes=[pltpu.SemaphoreType.DMA(())],
        compiler_params=_params("arbitrary"),
        name="moe_gather",
    )(row_tok.reshape(n_items, 1, MOE_ROWS), h.reshape(n, sub, LANES))
    return out.reshape(cap, d)


def _moe(h, logits, W, layer):
    n, d = h.shape
    top_val, top_idx = lax.top_k(logits, TOP_K)
    gates = jax.nn.softmax(top_val, axis=-1)
    flat_e = top_idx.reshape(-1)
    order = jnp.argsort(flat_e)
    sorted_e = flat_e[order]
    tok = (order // TOP_K).astype(jnp.int32)
    w_sorted = gates.reshape(-1)[order]
    counts = jnp.bincount(flat_e, length=N_EXPERTS)
    padded = (counts + MOE_ROWS - 1) // MOE_ROWS * MOE_ROWS
    pad_end = jnp.cumsum(padded)
    pad_start = pad_end - padded
    grp_start = jnp.cumsum(counts) - counts
    rank = jnp.arange(n * TOP_K) - grp_start[sorted_e]
    dest = pad_start[sorted_e] + rank
    n_items = -(-(n * TOP_K) // MOE_ROWS) + N_EXPERTS
    cap = n_items * MOE_ROWS
    row_tok = jnp.zeros((cap,), jnp.int32).at[dest].set(tok)
    row_w = jnp.zeros((cap,), F32).at[dest].set(w_sorted)
    block_e = jnp.minimum(jnp.searchsorted(pad_end, jnp.arange(n_items) * MOE_ROWS, side='right'),
                          N_EXPERTS - 1).astype(jnp.int32)
    first = jnp.concatenate([jnp.ones((1,), jnp.int32),
                             (block_e[1:] != block_e[:-1]).astype(jnp.int32)])
    xs = _gather_rows(h, row_tok)
    ys = _moe_ffn(xs, row_w, block_e, first, W['w_gu'], W['b_gu'], W['w_dn'], W['b_dn'], layer)
    return jnp.zeros((n, d), F32).at[row_tok].add(ys)


def _mixer_tail(proj, a, W, layer):
    hy = _hyena_branch(proj, W, layer)
    cvraw = _dwconv(proj, OFF_CV, CV_WIDTH, W['cv_dw_w'], W['cv_dw_b'], layer, glu=True)
    return _merge1(proj, a, hy, cvraw, W, layer)


def _trunk_layer(x, ctx, c8, W, ng3, layer, need_ctx):
    s = x.shape[0]
    lc = ctx.shape[0]
    lam_init = 0.8 - 0.6 * math.exp(-0.3 * layer)
    mod3 = _ada(c8, W['w_ada'], W['b_ada'], layer).reshape(-1, 1, D_MODEL)
    h_lat = _norm_mod(x, ng3, mod3, layer * 4, 1, 0)
    h_ctx = _norm_mod(ctx, ng3, mod3, layer * 4, 6 + 1, 6 + 0)
    attn = functools.partial(_diff_attention, da_lambda=W['da_lambda'], da_subln_g=W['da_subln_g'],
                             layer=layer, lam_init=lam_init)
    proj = _mm(h_lat, W['w_in'], layer)
    tabs = _rope_tables(s)
    if need_ctx:
        proj_c = _mm(h_ctx, W['w_in'], layer)
        kt_all, v_all = _kv_prep(proj, proj_c, OFF_K // K_COLS, tabs)
    else:
        kv_c = _mm(h_ctx, W['w_in'], layer, col_off=OFF_K, n_cols=K_COLS + V_COLS)
        kt_all, v_all = _kv_prep(proj, kv_c, 0, tabs)
    a_lat = attn(_q_prep(proj, tabs, rope=True), kt_all, v_all)
    x, f_lat, lg_lat = _merge2(_mixer_tail(proj, a_lat, W, layer), x, W, ng3, mod3, layer, 0)
    if need_ctx:
        a_ctx = attn(_q_prep(proj_c, tabs, rope=False), kt_all[:, :lc], v_all[:lc])
        ctx, f_ctx, lg_ctx = _merge2(_mixer_tail(proj_c, a_ctx, W, layer), ctx, W, ng3, mod3, layer, 1)
        f = _moe(jnp.concatenate([f_ctx, f_lat], axis=0), jnp.concatenate([lg_ctx, lg_lat], axis=0), W, layer)
        ctx = _resid_norm(ctx, f, ng3, mod3, layer * 4 + 3, 6 + 5)
        x = _resid_norm(x, f, ng3, mod3, layer * 4 + 3, 5, row_off=lc)
    else:
        f = _moe(f_lat, lg_lat, W, layer)
        x = _resid_norm(x, f, ng3, mod3, layer * 4 + 3, 5)
    return x, ctx


def kernel(x, c, ctx, c_ctx, w_ada, b_ada, norm_g, w_in, b_gate, da_lambda, da_subln_g, w_da_out, hy_short_w, hy_short_b, hy_w1, hy_b1, hy_w2, hy_b2, hy_freq, hy_w3, hy_bias, w_hy_out, cv_dw_w, cv_dw_b, cv_ln_g, cv_ln_b, w_cv_out, w_out, w_router, b_router, w_gu, b_gu, w_dn, b_dn):
    W = dict(w_ada=w_ada, b_ada=b_ada, w_in=w_in, b_gate=b_gate, da_lambda=da_lambda, da_subln_g=da_subln_g,
             w_da_out=w_da_out, hy_short_w=hy_short_w, hy_short_b=hy_short_b, hy_w1=hy_w1, hy_b1=hy_b1,
             hy_w2=hy_w2, hy_b2=hy_b2, hy_freq=hy_freq, hy_w3=hy_w3, hy_bias=hy_bias, w_hy_out=w_hy_out,
             cv_dw_w=cv_dw_w, cv_dw_b=cv_dw_b, cv_ln_g=cv_ln_g, cv_ln_b=cv_ln_b, w_cv_out=w_cv_out,
             w_out=w_out, w_router=w_router, b_router=b_router, w_gu=w_gu, b_gu=b_gu, w_dn=w_dn, b_dn=b_dn)
    assert x.shape[0] == 1 and ctx.shape[0] == 1
    xl = x[0]
    cl = ctx[0]
    c8 = jnp.concatenate([c, c_ctx[None], jnp.zeros((6, D_MODEL), F32)], axis=0)
    ng3 = norm_g.reshape(DEPTH * 4, 1, D_MODEL)
    for layer in range(DEPTH):
        xl, cl = _trunk_layer(xl, cl, c8, W, ng3, layer, layer < DEPTH - 1)
    return xl[None]
```

```python
import functools
import math

import numpy as np
import jax
import jax.numpy as jnp
from jax import lax
from jax.experimental import pallas as pl
from jax.experimental.pallas import tpu as pltpu

F32 = jnp.float32
BF16 = jnp.bfloat16

D_MODEL = 2048
DEPTH = 2
GRID_W = 64
EPS = 1e-6

DA_HEADS = 8
DA_QK_DIM = 64
DA_V_DIM = 2 * DA_QK_DIM
DA_WIDTH = DA_HEADS * DA_V_DIM
DA_SCALE = DA_QK_DIM ** -0.5
LOG2E = math.log2(math.e)
ROPE_BASE = 10000.0

HY_WIDTH = 1024
HY_ORDER = 2
HY_SHORT = 3
HY_EMB = 33
HY_BANDS = (HY_EMB - 1) // 2
HY_FFN = 64
HY_MIN_DECAY = math.log(1e-2) / 1.5
HY_MAX_DECAY = math.log(1e-2) / 0.3

CV_WIDTH = 1024
CV_KERNEL = 31
N_BRANCH = 3

N_EXPERTS = 32
TOP_K = 4
D_EXPERT = 1024
SWIGLU_LIMIT = 7.0
SWIGLU_ALPHA = 1.702

Q_COLS = 2 * DA_HEADS * DA_QK_DIM
K_COLS = Q_COLS
V_COLS = DA_WIDTH
HY_COLS = (HY_ORDER + 1) * HY_WIDTH
CV_COLS = 2 * CV_WIDTH
GATE_COLS = N_BRANCH * D_MODEL
OFF_Q = 0
OFF_K = OFF_Q + Q_COLS
OFF_V = OFF_K + K_COLS
OFF_HY = OFF_V + V_COLS
OFF_CV = OFF_HY + HY_COLS
OFF_GATE = OFF_CV + CV_COLS
IN_COLS = OFF_GATE + GATE_COLS

LANES = 128
VMEM_LIMIT_BYTES = 56 * 1024 * 1024
MOE_ROWS = 256
FFT_R = 128
FFT_C = 128
CONV_HALO = 16


def _params(*sem):
    return pltpu.CompilerParams(dimension_semantics=sem, vmem_limit_bytes=VMEM_LIMIT_BYTES)


def _once(shape, index_map):
    return pl.BlockSpec(shape, index_map, pipeline_mode=pl.Buffered(1))


def _split_bf16(a):
    hi = a.astype(BF16)
    return hi, (a - hi.astype(F32)).astype(BF16)


def _dot3(a, b):
    ah, al = _split_bf16(a)
    bh, bl = _split_bf16(b)
    d = functools.partial(jnp.dot, preferred_element_type=F32)
    return d(ah, bh) + (d(ah, bl) + d(al, bh))


def _rms(x):
    return x * lax.rsqrt(jnp.mean(x * x, axis=-1, keepdims=True) + EPS)


def _mm_kernel(x_ref, w_ref, o_ref, wb_ref):
    @pl.when(pl.program_id(1) == 0)
    def _():
        wb_ref[...] = w_ref[...].astype(BF16)

    o_ref[...] = jnp.dot(x_ref[...].astype(BF16), wb_ref[...],
                         preferred_element_type=F32).astype(o_ref.dtype)


def _mm(x, w, layer, *, col_off=0, n_cols=None, out_dtype=F32, bm=1024, bn=1024):
    m, k = x.shape
    n = w.shape[2] if n_cols is None else n_cols
    bm = min(bm, m)
    bn = min(bn, n)
    assert m % bm == 0 and n % bn == 0 and col_off % bn == 0
    joff = col_off // bn
    return pl.pallas_call(
        _mm_kernel,
        grid=(n // bn, m // bm),
        in_specs=[pl.BlockSpec((bm, k), lambda j, i: (i, 0)),
                  pl.BlockSpec((None, k, bn), lambda j, i: (layer, 0, j + joff))],
        out_specs=pl.BlockSpec((bm, bn), lambda j, i: (i, j)),
        out_shape=jax.ShapeDtypeStruct((m, n), out_dtype),
        scratch_shapes=[pltpu.VMEM((k, bn), BF16)],
        compiler_params=_params("arbitrary", "arbitrary"),
        name="dense_mm",
    )(x, w)


def _ada_kernel(c_ref, w_ref, b_ref, o_ref):
    c = c_ref[...]
    o_ref[...] = _dot3(c * jax.nn.sigmoid(c), w_ref[...]) + b_ref[...]


def _ada(c8, w_ada, b_ada, layer, *, bn=1024):
    r, d = c8.shape
    n = w_ada.shape[2]
    return pl.pallas_call(
        _ada_kernel,
        grid=(n // bn,),
        in_specs=[pl.BlockSpec((r, d), lambda j: (0, 0)),
                  pl.BlockSpec((None, d, bn), lambda j: (layer, 0, j)),
                  pl.BlockSpec((None, 1, bn), lambda j: (layer, 0, j))],
        out_specs=pl.BlockSpec((r, bn), lambda j: (0, j)),
        out_shape=jax.ShapeDtypeStruct((r, n), F32),
        compiler_params=_params("arbitrary"),
        name="ada_mod",
    )(c8, w_ada, b_ada.reshape(DEPTH, 1, n))


def _norm_mod_kernel(x_ref, g_ref, sc_ref, sh_ref, o_ref):
    y = _rms(x_ref[...]) * g_ref[...]
    o_ref[...] = (y * (1.0 + sc_ref[...]) + sh_ref[...]).astype(o_ref.dtype)


def _vec(idx):
    return pl.BlockSpec((None, 1, D_MODEL), lambda *_: (idx, 0, 0))


def _norm_mod(x, ng3, mod3, ng_idx, sc_idx, sh_idx, *, bm=512):
    m, d = x.shape
    bm = min(bm, m)
    return pl.pallas_call(
        _norm_mod_kernel,
        grid=(m // bm,),
        in_specs=[pl.BlockSpec((bm, d), lambda i: (i, 0)), _vec(ng_idx), _vec(sc_idx), _vec(sh_idx)],
        out_specs=pl.BlockSpec((bm, d), lambda i: (i, 0)),
        out_shape=jax.ShapeDtypeStruct((m, d), BF16),
        compiler_params=_params("arbitrary"),
        name="norm_mod",
    )(x, ng3, mod3, mod3)


def _resid_kernel(x_ref, f_ref, ng_ref, gate_ref, o_ref):
    o_ref[...] = x_ref[...] + gate_ref[...] * (_rms(f_ref[...]) * ng_ref[...])


def _resid_norm(x, f, ng3, mod3, ng_idx, gate_idx, *, row_off=0, bm=256):
    m, d = x.shape
    bm = min(bm, m)
    assert row_off % bm == 0
    roff = row_off // bm
    return pl.pallas_call(
        _resid_kernel,
        grid=(m // bm,),
        in_specs=[pl.BlockSpec((bm, d), lambda i: (i, 0)), pl.BlockSpec((bm, d), lambda i: (i + roff, 0)),
                  _vec(ng_idx), _vec(gate_idx)],
        out_specs=pl.BlockSpec((bm, d), lambda i: (i, 0)),
        out_shape=jax.ShapeDtypeStruct((m, d), F32),
        compiler_params=_params("arbitrary"),
        name="resid_norm",
    )(x, f, ng3, mod3)


def _rope_tile(x, cos, sa, sb):
    q4 = DA_QK_DIM // 4
    return x * cos + pltpu.roll(x, LANES - q4, 1) * sa + pltpu.roll(x, q4, 1) * sb


def _q_prep_kernel(p_ref, cos_ref, sa_ref, sb_ref, o_ref, *, rope):
    for h in range(DA_HEADS):
        sl = slice(h * LANES, (h + 1) * LANES)
        x = p_ref[:, sl]
        if rope:
            x = _rope_tile(x, cos_ref[...], sa_ref[...], sb_ref[...])
        o_ref[:, sl] = (x * (DA_SCALE * LOG2E)).astype(BF16)


def _q_prep(proj, tabs, *, rope, bm=256):
    m = proj.shape[0]
    bm = min(bm, m)
    tab = pl.BlockSpec((bm, LANES), lambda i: (i, 0))
    return pl.pallas_call(
        functools.partial(_q_prep_kernel, rope=rope),
        grid=(m // bm,),
        in_specs=[pl.BlockSpec((bm, Q_COLS), lambda i: (i, OFF_Q // Q_COLS)), tab, tab, tab],
        out_specs=pl.BlockSpec((bm, Q_COLS), lambda i: (i, 0)),
        out_shape=jax.ShapeDtypeStruct((m, Q_COLS), BF16),
        compiler_params=_params("arbitrary"),
        name="q_prep",
    )(proj, *tabs)


def _kv_prep_kernel(kc_ref, vc_ref, k_ref, v_ref, cos_ref, sa_ref, sb_ref, kt_ref, vo_ref):
    i = pl.program_id(0)

    @pl.when(i == 0)
    def _():
        for h in range(DA_HEADS):
            sl = slice(h * LANES, (h + 1) * LANES)
            kt_ref[sl, :] = kc_ref[:, sl].T.astype(BF16)
        vo_ref[...] = vc_ref[...].astype(BF16)

    @pl.when(i > 0)
    def _():
        for h in range(DA_HEADS):
            sl = slice(h * LANES, (h + 1) * LANES)
            kt_ref[sl, :] = _rope_tile(k_ref[:, sl], cos_ref[...], sa_ref[...], sb_ref[...]).T.astype(BF16)
        vo_ref[...] = v_ref[...].astype(BF16)


def _kv_prep(proj, ctx_kv, ctx_k_blk, tabs):
    s = proj.shape[0]
    lc = ctx_kv.shape[0]
    assert s % lc == 0
    nb = s // lc + 1
    lat = lambda i: (jnp.maximum(i - 1, 0), 0)
    tab = pl.BlockSpec((lc, LANES), lat)
    return pl.pallas_call(
        _kv_prep_kernel,
        grid=(nb,),
        in_specs=[pl.BlockSpec((lc, K_COLS), lambda i: (0, ctx_k_blk)),
                  pl.BlockSpec((lc, V_COLS), lambda i: (0, ctx_k_blk + 1)),
                  pl.BlockSpec((lc, K_COLS), lambda i: (jnp.maximum(i - 1, 0), OFF_K // K_COLS)),
                  pl.BlockSpec((lc, V_COLS), lambda i: (jnp.maximum(i - 1, 0), OFF_V // V_COLS)),
                  tab, tab, tab],
        out_specs=[pl.BlockSpec((K_COLS, lc), lambda i: (0, i)),
                   pl.BlockSpec((lc, V_COLS), lambda i: (i, 0))],
        out_shape=[jax.ShapeDtypeStruct((K_COLS, s + lc), BF16),
                   jax.ShapeDtypeStruct((s + lc, V_COLS), BF16)],
        compiler_params=_params("arbitrary"),
        name="kv_prep",
    )(ctx_kv, ctx_kv, proj, proj, *tabs)


def _rope_tables(n_lat):
    rows = n_lat // GRID_W
    row = jnp.repeat(jnp.arange(rows, dtype=F32), GRID_W)
    col = jnp.tile(jnp.arange(GRID_W, dtype=F32), rows)
    half = DA_QK_DIM // 2
    inv = ROPE_BASE ** (-jnp.arange(0, half, 2, dtype=F32) / half)
    ar = row[:, None] * inv
    ac = col[:, None] * inv
    zr = jnp.zeros_like(ar)
    cos = jnp.concatenate([jnp.cos(ar), jnp.cos(ar), jnp.cos(ac), jnp.cos(ac)], axis=-1)
    sa = jnp.concatenate([-jnp.sin(ar), zr, -jnp.sin(ac), zr], axis=-1)
    sb = jnp.concatenate([zr, jnp.sin(ar), zr, jnp.sin(ac)], axis=-1)
    return tuple(jnp.tile(t, (1, LANES // DA_QK_DIM)) for t in (cos, sa, sb))


def _attn_kernel(lp_ref, g_ref, q_ref, kt_ref, v_ref, o_ref, *, bk, lam_init):
    bq = q_ref.shape[0]
    nk = kt_ref.shape[1] // bk
    q = q_ref[...]
    lane = lax.broadcasted_iota(jnp.int32, q.shape, 1)
    zero = jnp.zeros_like(q)
    q2 = jnp.concatenate([jnp.where(lane < DA_QK_DIM, q, zero),
                          jnp.where(lane >= DA_QK_DIM, q, zero)], axis=0)

    def body(i, carry):
        m, l, acc = carry
        off = pl.multiple_of(i * bk, bk)
        s = jnp.dot(q2, kt_ref[:, pl.ds(off, bk)], preferred_element_type=F32)
        m_new = jnp.maximum(m, jnp.max(s, axis=-1, keepdims=True))
        alpha = jnp.exp2(m - m_new)
        p = jnp.exp2(s - m_new)
        l = alpha * l + jnp.sum(p, axis=-1, keepdims=True)
        acc = alpha * acc + jnp.dot(p.astype(BF16), v_ref[pl.ds(off, bk), :],
                                    preferred_element_type=F32)
        return m_new, l, acc

    m0 = jnp.full((2 * bq, 1), -jnp.inf, F32)
    l0 = jnp.zeros((2 * bq, 1), F32)
    a0 = jnp.zeros((2 * bq, DA_V_DIM), F32)
    _, l, acc = lax.fori_loop(0, nk, body, (m0, l0, a0))
    o = acc / l
    lp = lp_ref[...].astype(F32)
    lam = (jnp.exp(jnp.sum(lp[0:1] * lp[1:2], axis=-1, keepdims=True))
           - jnp.exp(jnp.sum(lp[2:3] * lp[3:4], axis=-1, keepdims=True)) + lam_init)
    o = o[:bq] - lam * o[bq:]
    y = _rms(o) * g_ref[...].astype(F32) * (1.0 - lam_init)
    o_ref[...] = y.astype(o_ref.dtype)


def _diff_attention(q, kt, v, da_lambda, da_subln_g, layer, lam_init, *, bq=256, bk=None):
    sq = q.shape[0]
    sk = v.shape[0]
    bq = min(bq, sq)
    if bk is None:
        bk = 1408 if sk % 1408 == 0 else sk
    hw = 2 * DA_QK_DIM
    return pl.pallas_call(
        functools.partial(_attn_kernel, bk=bk, lam_init=lam_init),
        grid=(DA_HEADS, sq // bq),
        in_specs=[pl.BlockSpec((None, 4, DA_QK_DIM), lambda h, i: (layer, 0, 0)),
                  pl.BlockSpec((None, 1, DA_V_DIM), lambda h, i: (layer, 0, 0)),
                  pl.BlockSpec((bq, hw), lambda h, i: (i, h)),
                  pl.BlockSpec((hw, sk), lambda h, i: (h, 0)),
                  pl.BlockSpec((sk, DA_V_DIM), lambda h, i: (0, h))],
        out_specs=pl.BlockSpec((bq, DA_V_DIM), lambda h, i: (i, h)),
        out_shape=jax.ShapeDtypeStruct((sq, DA_WIDTH), BF16),
        compiler_params=_params("arbitrary", "arbitrary"),
        name="diff_attention",
    )(da_lambda, da_subln_g.reshape(DEPTH, 1, DA_V_DIM), q, kt, v)


def _dwconv_kernel(*refs, ntaps, glu, n_row_blocks):
    if glu:
        cur, prv, nxt, gcur, gprv, gnxt, w_ref, b_ref, o_ref, pad_ref = refs
    else:
        cur, prv, nxt, w_ref, b_ref, o_ref, pad_ref = refs
        gcur = gprv = gnxt = None
    i = pl.program_id(1)
    t, cw = cur.shape
    h = CONV_HALO
    half = (ntaps - 1) // 2

    def val(a, g):
        return a[...] * jax.nn.sigmoid(g[...]) if glu else a[...]

    pad_ref[pl.ds(h, t), :] = val(cur, gcur)
    pad_ref[pl.ds(0, h), :] = jnp.where(i > 0, val(prv, gprv), 0.0)
    pad_ref[pl.ds(h + t, h), :] = jnp.where(i < n_row_blocks - 1, val(nxt, gnxt), 0.0)
    rc = 64
    for r in range(0, t, rc):
        for c in range(0, cw, LANES):
            acc = jnp.broadcast_to(b_ref[:, c:c + LANES], (rc, LANES))
            for j in range(ntaps):
                acc = acc + w_ref[j:j + 1, c:c + LANES] * pad_ref[pl.ds(r + h - half + j, rc), c:c + LANES]
            o_ref[pl.ds(r, rc), c:c + LANES] = acc


def _dwconv(u, col_off, n_ch, w, b, layer, *, glu=False, t=512, cw=256):
    s = u.shape[0]
    ntaps = w.shape[1]
    t = min(t, s)
    assert s % t == 0 and n_ch % cw == 0 and col_off % cw == 0 and t % CONV_HALO == 0
    nrb = s // t
    hb = t // CONV_HALO
    last_h = s // CONV_HALO - 1
    coff = col_off // cw
    goff = (col_off + n_ch) // cw

    def specs(off):
        return [pl.BlockSpec((t, cw), lambda c, i: (i, c + off)),
                pl.BlockSpec((CONV_HALO, cw), lambda c, i: (jnp.maximum(i * hb - 1, 0), c + off)),
                pl.BlockSpec((CONV_HALO, cw), lambda c, i: (jnp.minimum((i + 1) * hb, last_h), c + off))]

    in_specs = specs(coff) + (specs(goff) if glu else [])
    args = [u] * (6 if glu else 3)
    in_specs += [pl.BlockSpec((None, ntaps, cw), lambda c, i: (layer, 0, c)),
                 pl.BlockSpec((None, 1, cw), lambda c, i: (layer, 0, c))]
    return pl.pallas_call(
        functools.partial(_dwconv_kernel, ntaps=ntaps, glu=glu, n_row_blocks=nrb),
        grid=(n_ch // cw, nrb),
        in_specs=in_specs,
        out_specs=pl.BlockSpec((t, cw), lambda c, i: (i, c)),
        out_shape=jax.ShapeDtypeStruct((s, n_ch), F32),
        scratch_shapes=[pltpu.VMEM((t + 2 * CONV_HALO, cw), F32)],
        compiler_params=_params("arbitrary", "arbitrary"),
        name="dwconv",
    )(*args, w, b.reshape(DEPTH, 1, n_ch))


def _hy_filter_kernel(z_ref, w1_ref, b1_ref, w2_ref, b2_ref, fr_ref, w3_ref, dl_ref, h_ref, s_ref):
    z = z_ref[...]
    fr = fr_ref[...]
    a = jnp.sin(fr * (_dot3(z, w1_ref[...]) + b1_ref[...]))
    a = jnp.sin(fr * (_dot3(a, w2_ref[...]) + b2_ref[...]))
    hh = _dot3(a, w3_ref[...]) * jnp.exp(-z[:, 0:1] * dl_ref[...])
    h_ref[...] = hh

    @pl.when(pl.program_id(0) == 0)
    def _():
        s_ref[...] = jnp.zeros_like(s_ref)

    s_ref[...] += jnp.sum(jnp.abs(hh), axis=0, keepdims=True)


def _hy_filters(n, hy_w1, hy_b1, hy_w2, hy_b2, hy_freq, hy_w3, layer, *, t=512):
    t = min(t, n)
    tt = jnp.linspace(0.0, 1.0, n, dtype=F32)[:, None]
    w = 2.0 * math.pi * jnp.arange(n, dtype=F32)[:, None] / n
    bands = jnp.linspace(1e-4, HY_BANDS - 1, HY_BANDS, dtype=F32)[None, :]
    z = jnp.concatenate([tt, jnp.cos(bands * w), -jnp.sin(bands * w),
                         jnp.zeros((n, LANES - HY_EMB), F32)], axis=-1)
    w1 = jnp.pad(hy_w1, ((0, 0), (0, LANES - HY_EMB), (0, 0)))
    deltas = jnp.abs(jnp.linspace(HY_MIN_DECAY, HY_MAX_DECAY, HY_WIDTH, dtype=F32))
    nf = HY_ORDER * 2 * HY_WIDTH
    dl = jnp.tile(deltas, HY_ORDER * 2)[None, :]
    row = lambda a: a.reshape(DEPTH, 1, HY_FFN)
    small = pl.BlockSpec((None, 1, HY_FFN), lambda i: (layer, 0, 0))
    return pl.pallas_call(
        _hy_filter_kernel,
        grid=(n // t,),
        in_specs=[pl.BlockSpec((t, LANES), lambda i: (i, 0)),
                  pl.BlockSpec((None, LANES, HY_FFN), lambda i: (layer, 0, 0)), small,
                  pl.BlockSpec((None, HY_FFN, HY_FFN), lambda i: (layer, 0, 0)), small, small,
                  pl.BlockSpec((None, HY_FFN, nf), lambda i: (layer, 0, 0)),
                  pl.BlockSpec((1, nf), lambda i: (0, 0))],
        out_specs=[pl.BlockSpec((t, nf), lambda i: (i, 0)), pl.BlockSpec((1, nf), lambda i: (0, 0))],
        out_shape=[jax.ShapeDtypeStruct((n, nf), F32), jax.ShapeDtypeStruct((1, nf), F32)],
        compiler_params=_params("arbitrary"),
        name="hy_filters",
    )(z, w1, row(hy_b1), hy_w2, row(hy_b2), row(hy_freq), hy_w3, dl)


def _angles(num, den):
    return 2.0 * np.pi * (np.asarray(num, np.int64) % den).astype(np.float64) / den


@functools.lru_cache(maxsize=None)
def _fft_tables():
    r = FFT_R
    n_fft = r * r
    hr = r // 2
    s2 = np.arange(r)[:, None, None]
    f1 = np.arange(r)[None, :, None]
    s1 = np.arange(hr)[None, None, :]
    th = _angles(f1 * (r * s1 + s2), n_fft)
    fwd = np.concatenate([np.cos(th), -np.sin(th)], axis=1)
    thb = _angles(f1 * (r * (r - 1 - s1) + s2), n_fft)
    bwd = np.concatenate([np.cos(thb), -np.sin(thb)], axis=1)
    fwd = np.concatenate([fwd, bwd], axis=2)
    t2 = _angles(np.arange(r)[:, None] * np.arange(r)[None, :], r)
    c, s = np.cos(t2), np.sin(t2)
    m2 = np.block([[c, s], [-s, c]])
    m3 = np.block([[c, -s], [s, c]])
    tt2 = np.arange(r)[:, None, None]
    tt1 = np.arange(hr)[None, :, None]
    ff1 = np.arange(r)[None, None, :]
    ph = _angles(ff1 * (r * tt1 + tt2), n_fft)
    g4 = np.concatenate([np.cos(ph), -np.sin(ph)], axis=2) / n_fft
    return tuple(jnp.asarray(a, dtype=BF16) for a in (fwd, m2, m3, g4))


def _store_stage1(a_ref, s2, out):
    r = FFT_R
    a_ref[pl.ds(s2, r, stride=2 * r), :] = out[:r]
    a_ref[pl.ds(s2 + r, r, stride=2 * r), :] = out[r:]


def _hy_spec_kernel(hf_ref, hb_ref, inv_ref, fwd_ref, m2_ref, k_ref, a_ref):
    r = FFT_R
    hr = r // 2

    def stage1(s2, c):
        xf = hf_ref[pl.ds(s2, hr, stride=r), :]
        xb = hb_ref[pl.ds(r - 1 - s2, hr, stride=r), :]
        xs = jnp.concatenate([xf, xb], axis=0).astype(BF16)
        _store_stage1(a_ref, s2, jnp.dot(fwd_ref[s2], xs, preferred_element_type=F32))
        return c

    lax.fori_loop(0, r, stage1, 0, unroll=4)

    def stage2(f1, c):
        r0 = pl.multiple_of(f1 * 2 * r, 2 * r)
        z = jnp.dot(m2_ref[...], a_ref[pl.ds(r0, 2 * r), :].astype(BF16), preferred_element_type=F32)
        k_ref[f1] = (z * inv_ref[...]).astype(k_ref.dtype)
        return c

    lax.fori_loop(0, r, stage2, 0, unroll=4)


def _hy_spectra(h, inv_norm, tables):
    n = h.shape[0]
    r = FFT_R
    assert 2 * n == r * r
    fwd, m2, _, _ = tables
    nc = HY_WIDTH // FFT_C
    return pl.pallas_call(
        _hy_spec_kernel,
        grid=(HY_ORDER, nc),
        in_specs=[_once((n, FFT_C), lambda o, c: (0, o * 2 * nc + c)),
                  _once((n, FFT_C), lambda o, c: (0, o * 2 * nc + nc + c)),
                  pl.BlockSpec((1, FFT_C), lambda o, c: (0, o * nc + c)),
                  _once((r, 2 * r, r), lambda o, c: (0, 0, 0)),
                  _once((2 * r, 2 * r), lambda o, c: (0, 0))],
        out_specs=pl.BlockSpec((None, None, r, 2 * r, FFT_C), lambda o, c: (o, c, 0, 0, 0)),
        out_shape=jax.ShapeDtypeStruct((HY_ORDER, nc, r, 2 * r, FFT_C), BF16),
        scratch_shapes=[pltpu.VMEM((2 * r * r, FFT_C), F32)],
        compiler_params=_params("arbitrary", "arbitrary"),
        name="hy_spectra",
    )(h, h, inv_norm, fwd, m2)


def _hy_conv_kernel(v_ref, x_ref, k_ref, bias_ref, fwd_ref, m2_ref, m3_ref, g4_ref, o_ref, a_ref):
    r = FFT_R
    hr = r // 2

    def stage1(s2, c):
        xs = v_ref[pl.ds(s2, hr, stride=r), :].astype(BF16)
        _store_stage1(a_ref, s2, jnp.dot(fwd_ref[s2][:, :hr], xs, preferred_element_type=F32))
        return c

    lax.fori_loop(0, r, stage1, 0, unroll=4)

    def stage23(f1, c):
        r0 = pl.multiple_of(f1 * 2 * r, 2 * r)
        z = jnp.dot(m2_ref[...], a_ref[pl.ds(r0, 2 * r), :].astype(BF16), preferred_element_type=F32)
        kk = k_ref[f1].astype(F32)
        zr, zi, kr, ki = z[:r], z[r:], kk[:r], kk[r:]
        y = jnp.concatenate([zr * kr - zi * ki, zr * ki + zi * kr], axis=0).astype(BF16)
        a_ref[pl.ds(r0, 2 * r), :] = jnp.dot(m3_ref[...], y, preferred_element_type=F32)
        return c

    lax.fori_loop(0, r, stage23, 0, unroll=4)

    def stage4(t2, c):
        b = jnp.concatenate([a_ref[pl.ds(t2, r, stride=2 * r), :],
                             a_ref[pl.ds(t2 + r, r, stride=2 * r), :]], axis=0).astype(BF16)
        y = jnp.dot(g4_ref[t2], b, preferred_element_type=F32)
        vs = v_ref[pl.ds(t2, hr, stride=r), :]
        xs = x_ref[pl.ds(t2, hr, stride=r), :]
        o_ref[pl.ds(t2, hr, stride=r), :] = (xs * (y + vs * bias_ref[...])).astype(o_ref.dtype)
        return c

    lax.fori_loop(0, r, stage4, 0, unroll=4)


def _hy_conv(zin, zin_off, xmul, xmul_off, spectra, order, hy_bias, layer, tables, out_dtype):
    n = zin.shape[0]
    r = FFT_R
    assert 2 * n == r * r and zin_off % FFT_C == 0 and xmul_off % FFT_C == 0
    fwd, m2, m3, g4 = tables
    nc = HY_WIDTH // FFT_C
    hr = r // 2
    zo = zin_off // FFT_C
    xo = xmul_off // FFT_C
    return pl.pallas_call(
        _hy_conv_kernel,
        grid=(nc,),
        in_specs=[_once((n, FFT_C), lambda c: (0, c + zo)),
                  _once((n, FFT_C), lambda c: (0, c + xo)),
                  _once((None, None, r, 2 * r, FFT_C), lambda c: (order, c, 0, 0, 0)),
                  pl.BlockSpec((None, 1, FFT_C), lambda c: (layer * HY_ORDER + order, 0, c)),
                  _once((r, 2 * r, r), lambda c: (0, 0, 0)),
                  _once((2 * r, 2 * r), lambda c: (0, 0)),
                  _once((2 * r, 2 * r), lambda c: (0, 0)),
                  _once((r, hr, 2 * r), lambda c: (0, 0, 0))],
        out_specs=pl.BlockSpec((n, FFT_C), lambda c: (0, c)),
        out_shape=jax.ShapeDtypeStruct((n, HY_WIDTH), out_dtype),
        scratch_shapes=[pltpu.VMEM((2 * r * r, FFT_C), F32)],
        compiler_params=_params("arbitrary"),
        name="hy_conv",
    )(zin, xmul, spectra, hy_bias.reshape(DEPTH * HY_ORDER, 1, HY_WIDTH), fwd, m2, m3, g4)


@functools.lru_cache(maxsize=None)
def _dft_tables(n):
    n_fft = 2 * n
    f = np.arange(n_fft)[:, None]
    s = np.arange(n)[None, :]
    th = _angles(f * s, n_fft)
    fz = np.concatenate([np.cos(th), -np.sin(th)], axis=0)
    thb = _angles(f * (n_fft - 1 - s), n_fft)
    fb = np.concatenate([np.cos(thb), -np.sin(thb)], axis=0)
    ph = _angles(np.arange(n)[:, None] * np.arange(n_fft)[None, :], n_fft)
    g = np.concatenate([np.cos(ph), -np.sin(ph)], axis=1) / n_fft
    return tuple(jnp.asarray(a, dtype=BF16) for a in (fz, fb, g))


def _hy_conv_small_kernel(v_ref, x_ref, hf_ref, hb_ref, inv_ref, bias_ref, fz_ref, fb_ref, g_ref, o_ref):
    d = functools.partial(jnp.dot, preferred_element_type=F32)
    v = v_ref[...]
    nf = fz_ref.shape[0] // 2
    z = d(fz_ref[...], v.astype(BF16))
    k = (d(fz_ref[...], hf_ref[...].astype(BF16)) + d(fb_ref[...], hb_ref[...].astype(BF16))) * inv_ref[...]
    zr, zi, kr, ki = z[:nf], z[nf:], k[:nf], k[nf:]
    y = jnp.concatenate([zr * kr - zi * ki, zr * ki + zi * kr], axis=0).astype(BF16)
    o_ref[...] = (x_ref[...] * (d(g_ref[...], y) + v * bias_ref[...])).astype(o_ref.dtype)


def _hy_conv_small(zin, zin_off, xmul, xmul_off, h, inv_norm, order, hy_bias, layer, out_dtype, *, cw=256):
    n = zin.shape[0]
    fz, fb, g = _dft_tables(n)
    nc = HY_WIDTH // cw
    zo = zin_off // cw
    xo = xmul_off // cw
    full = lambda a: pl.BlockSpec(a.shape, lambda c: (0, 0))
    return pl.pallas_call(
        _hy_conv_small_kernel,
        grid=(nc,),
        in_specs=[pl.BlockSpec((n, cw), lambda c: (0, c + zo)),
                  pl.BlockSpec((n, cw), lambda c: (0, c + xo)),
                  pl.BlockSpec((n, cw), lambda c: (0, order * 2 * nc + c)),
                  pl.BlockSpec((n, cw), lambda c: (0, order * 2 * nc + nc + c)),
                  pl.BlockSpec((1, cw), lambda c: (0, order * nc + c)),
                  pl.BlockSpec((None, 1, cw), lambda c: (layer * HY_ORDER + order, 0, c)),
                  full(fz), full(fb), full(g)],
        out_specs=pl.BlockSpec((n, cw), lambda c: (0, c)),
        out_shape=jax.ShapeDtypeStruct((n, HY_WIDTH), out_dtype),
        compiler_params=_params("arbitrary"),
        name="hy_conv_small",
    )(zin, xmul, h, h, inv_norm, hy_bias.reshape(DEPTH * HY_ORDER, 1, HY_WIDTH), fz, fb, g)


def _hyena_branch(proj, W, layer):
    n = proj.shape[0]
    u = _dwconv(proj, OFF_HY, HY_COLS, W['hy_short_w'], W['hy_short_b'], layer)
    h, hsum = _hy_filters(n, W['hy_w1'], W['hy_b1'], W['hy_w2'], W['hy_b2'], W['hy_freq'], W['hy_w3'], layer)
    inv_norm = (1.0 / jnp.sum(hsum.reshape(HY_ORDER, 2, HY_WIDTH), axis=1)).reshape(1, HY_ORDER * HY_WIDTH)
    if 2 * n == FFT_R * FFT_R:
        tables = _fft_tables()
        spectra = _hy_spectra(h, inv_norm, tables)
        z1 = _hy_conv(u, 2 * HY_WIDTH, u, 0, spectra, 0, W['hy_bias'], layer, tables, F32)
        return _hy_conv(z1, 0, u, HY_WIDTH, spectra, 1, W['hy_bias'], layer, tables, F32)
    z1 = _hy_conv_small(u, 2 * HY_WIDTH, u, 0, h, inv_norm, 0, W['hy_bias'], layer, F32)
    return _hy_conv_small(z1, 0, u, HY_WIDTH, h, inv_norm, 1, W['hy_bias'], layer, F32)


def _merge1_kernel(a_ref, hy_ref, cv_ref, g0_ref, g1_ref, g2_ref, bg0_ref, bg1_ref, bg2_ref,
                   wa_ref, wh_ref, wc_ref, lng_ref, lnb_ref, o_ref, wab_ref, whb_ref, wcb_ref):
    @pl.when(pl.program_id(1) == 0)
    def _():
        wab_ref[...] = wa_ref[...].astype(BF16)
        whb_ref[...] = wh_ref[...].astype(BF16)
        wcb_ref[...] = wc_ref[...].astype(BF16)

    d = functools.partial(jnp.dot, preferred_element_type=F32)
    cv = cv_ref[...]
    mu = jnp.mean(cv, axis=-1, keepdims=True)
    var = jnp.mean(jnp.square(cv - mu), axis=-1, keepdims=True)
    cv = (cv - mu) * lax.rsqrt(var + EPS) * lng_ref[...] + lnb_ref[...]
    cv = cv * jax.nn.sigmoid(cv)
    y = (jax.nn.sigmoid(g0_ref[...] + bg0_ref[...]) * d(a_ref[...], wab_ref[...])
         + jax.nn.sigmoid(g1_ref[...] + bg1_ref[...]) * d(hy_ref[...].astype(BF16), whb_ref[...])
         + jax.nn.sigmoid(g2_ref[...] + bg2_ref[...]) * d(cv.astype(BF16), wcb_ref[...]))
    o_ref[...] = y.astype(o_ref.dtype)


def _merge1(proj, a, hy, cvraw, W, layer, *, bm=512, bn=512):
    m = proj.shape[0]
    bm = min(bm, m)
    d = D_MODEL
    nj = d // bn
    gate = lambda b: pl.BlockSpec((bm, bn), lambda j, i: (i, (OFF_GATE + b * d) // bn + j))
    bias = lambda b: pl.BlockSpec((None, 1, bn), lambda j, i: (layer, 0, b * nj + j))
    wspec = lambda k: pl.BlockSpec((None, k, bn), lambda j, i: (layer, 0, j))
    act = lambda k: pl.BlockSpec((bm, k), lambda j, i: (i, 0))
    lnv = pl.BlockSpec((None, 1, CV_WIDTH), lambda j, i: (layer, 0, 0))
    return pl.pallas_call(
        _merge1_kernel,
        grid=(nj, m // bm),
        in_specs=[act(DA_WIDTH), act(HY_WIDTH), act(CV_WIDTH), gate(0), gate(1), gate(2),
                  bias(0), bias(1), bias(2), wspec(DA_WIDTH), wspec(HY_WIDTH), wspec(CV_WIDTH), lnv, lnv],
        out_specs=pl.BlockSpec((bm, bn), lambda j, i: (i, j)),
        out_shape=jax.ShapeDtypeStruct((m, d), BF16),
        scratch_shapes=[pltpu.VMEM((DA_WIDTH, bn), BF16), pltpu.VMEM((HY_WIDTH, bn), BF16),
                        pltpu.VMEM((CV_WIDTH, bn), BF16)],
        compiler_params=_params("arbitrary", "arbitrary"),
        name="merge_branches",
    )(a, hy, cvraw, proj, proj, proj, *([W['b_gate'].reshape(DEPTH, 1, GATE_COLS)] * 3),
      W['w_da_out'], W['w_hy_out'], W['w_cv_out'],
      W['cv_ln_g'].reshape(DEPTH, 1, CV_WIDTH), W['cv_ln_b'].reshape(DEPTH, 1, CV_WIDTH))


def _merge2_kernel(y_ref, x_ref, w_ref, ng1_ref, g1_ref, ng2_ref, sc2_ref, sh2_ref, wr_ref, br_ref,
                   xo_ref, f_ref, lg_ref, wb_ref):
    @pl.when(pl.program_id(0) == 0)
    def _():
        wb_ref[...] = w_ref[...].astype(BF16)

    m = jnp.dot(y_ref[...], wb_ref[...], preferred_element_type=F32)
    xn = x_ref[...] + g1_ref[...] * (_rms(m) * ng1_ref[...])
    xo_ref[...] = xn
    f = _rms(xn) * ng2_ref[...] * (1.0 + sc2_ref[...]) + sh2_ref[...]
    f_ref[...] = f.astype(f_ref.dtype)
    lg_ref[...] = _dot3(f, wr_ref[...]) + br_ref[...]


def _merge2(y, x, W, ng3, mod3, layer, mod_row, *, bm=256):
    m, d = x.shape
    bm = min(bm, m)
    row = pl.BlockSpec((bm, d), lambda i: (i, 0))
    return pl.pallas_call(
        _merge2_kernel,
        grid=(m // bm,),
        in_specs=[row, row, _once((None, d, d), lambda i: (layer, 0, 0)),
                  _vec(layer * 4 + 1), _vec(mod_row * 6 + 2), _vec(layer * 4 + 2),
                  _vec(mod_row * 6 + 4), _vec(mod_row * 6 + 3),
                  pl.BlockSpec((None, d, N_EXPERTS), lambda i: (layer, 0, 0)),
                  pl.BlockSpec((None, 1, N_EXPERTS), lambda i: (layer, 0, 0))],
        out_specs=[row, row, pl.BlockSpec((bm, N_EXPERTS), lambda i: (i, 0))],
        out_shape=[jax.ShapeDtypeStruct((m, d), F32), jax.ShapeDtypeStruct((m, d), BF16),
                   jax.ShapeDtypeStruct((m, N_EXPERTS), F32)],
        scratch_shapes=[pltpu.VMEM((d, d), BF16)],
        compiler_params=_params("arbitrary"),
        name="out_proj_norm_router",
    )(y, x, W['w_out'], ng3, mod3, ng3, mod3, mod3, W['w_router'],
      W['b_router'].reshape(DEPTH, 1, N_EXPERTS))


def _moe_gu_kernel(be_ref, first_ref, x_ref, wg_ref, wu_ref, bg_ref, bu_ref, o_ref, wgb_ref, wub_ref):
    it = pl.program_id(1)

    @pl.when(first_ref[it] == 1)
    def _():
        wgb_ref[...] = wg_ref[...].astype(BF16)
        wub_ref[...] = wu_ref[...].astype(BF16)

    x = x_ref[...]
    gate = jnp.dot(x, wgb_ref[...], preferred_element_type=F32) + bg_ref[...]
    up = jnp.dot(x, wub_ref[...], preferred_element_type=F32) + bu_ref[...]
    gate = jnp.minimum(gate, SWIGLU_LIMIT)
    up = jnp.clip(up, -SWIGLU_LIMIT, SWIGLU_LIMIT)
    act = gate * jax.nn.sigmoid(SWIGLU_ALPHA * gate) * (up + 1.0)
    o_ref[...] = act.astype(o_ref.dtype)


def _moe_dn_kernel(be_ref, first_ref, a_ref, w_ref, b_ref, rw_ref, o_ref, wb_ref):
    it = pl.program_id(0)

    @pl.when(first_ref[it] == 1)
    def _():
        wb_ref[...] = w_ref[...].astype(BF16)

    y = jnp.dot(a_ref[...], wb_ref[...], preferred_element_type=F32) + b_ref[...]
    o_ref[...] = (y * rw_ref[...]).astype(o_ref.dtype)


def _moe_ffn(xs, row_w, block_e, first, w_gu, b_gu, w_dn, b_dn, layer, *, bn=512):
    cap, d = xs.shape
    n_items = cap // MOE_ROWS
    nj = D_EXPERT // bn
    act = pl.pallas_call(
        _moe_gu_kernel,
        grid_spec=pltpu.PrefetchScalarGridSpec(
            num_scalar_prefetch=2,
            grid=(nj, n_items),
            in_specs=[pl.BlockSpec((MOE_ROWS, d), lambda j, i, be, fi: (i, 0)),
                      pl.BlockSpec((None, None, d, bn), lambda j, i, be, fi: (layer, be[i], 0, j)),
                      pl.BlockSpec((None, None, d, bn), lambda j, i, be, fi: (layer, be[i], 0, j + nj)),
                      pl.BlockSpec((None, None, 1, bn), lambda j, i, be, fi: (layer, be[i], 0, j)),
                      pl.BlockSpec((None, None, 1, bn), lambda j, i, be, fi: (layer, be[i], 0, j + nj))],
            out_specs=pl.BlockSpec((MOE_ROWS, bn), lambda j, i, be, fi: (i, j)),
            scratch_shapes=[pltpu.VMEM((d, bn), BF16), pltpu.VMEM((d, bn), BF16)]),
        out_shape=jax.ShapeDtypeStruct((cap, D_EXPERT), BF16),
        compiler_params=_params("arbitrary", "arbitrary"),
        name="moe_gate_up",
    )(block_e, first, xs, w_gu, w_gu, b_gu.reshape(DEPTH, N_EXPERTS, 1, 2 * D_EXPERT),
      b_gu.reshape(DEPTH, N_EXPERTS, 1, 2 * D_EXPERT))
    return pl.pallas_call(
        _moe_dn_kernel,
        grid_spec=pltpu.PrefetchScalarGridSpec(
            num_scalar_prefetch=2,
            grid=(n_items,),
            in_specs=[pl.BlockSpec((MOE_ROWS, D_EXPERT), lambda i, be, fi: (i, 0)),
                      pl.BlockSpec((None, None, D_EXPERT, d), lambda i, be, fi: (layer, be[i], 0, 0)),
                      pl.BlockSpec((None, None, 1, d), lambda i, be, fi: (layer, be[i], 0, 0)),
                      pl.BlockSpec((MOE_ROWS, 1), lambda i, be, fi: (i, 0))],
            out_specs=pl.BlockSpec((MOE_ROWS, d), lambda i, be, fi: (i, 0)),
            scratch_shapes=[pltpu.VMEM((D_EXPERT, d), BF16)]),
        out_shape=jax.ShapeDtypeStruct((cap, d), F32),
        compiler_params=_params("arbitrary"),
        name="moe_down",
    )(block_e, first, act, w_dn, b_dn.reshape(DEPTH, N_EXPERTS, 1, d), row_w.reshape(cap, 1))


def _gather_kernel(tok_ref, h_ref, o_ref, sem):
    def row_copy(r, tok):
        return pltpu.make_async_copy(h_ref.at[tok], o_ref.at[r], sem)

    def issue(r, c):
        row_copy(r, tok_ref[0, r]).start()
        return c

    lax.fori_loop(0, MOE_ROWS, issue, 0)

    def drain(r, c):
        row_copy(r, 0).wait()
        return c

    lax.fori_loop(0, MOE_ROWS, drain, 0)


def _gather_rows(h, row_tok):
    n, d = h.shape
    cap = row_tok.shape[0]
    n_items = cap // MOE_ROWS
    sub = d // LANES
    out = pl.pallas_call(
        _gather_kernel,
        grid=(n_items,),
        in_specs=[pl.BlockSpec((None, 1, MOE_ROWS), lambda i: (i, 0, 0), memory_space=pltpu.SMEM),
                  pl.BlockSpec(memory_space=pl.ANY)],
        out_specs=pl.BlockSpec((MOE_ROWS, sub, LANES), lambda i: (i, 0, 0)),
        out_shape=jax.ShapeDtypeStruct((cap, sub, LANES), h.dtype),
        scratch_shapes=[pltpu.SemaphoreType.DMA(())],
        compiler_params=_params("arbitrary"),
        name="moe_gather",
    )(row_tok.reshape(n_items, 1, MOE_ROWS), h.reshape(n, sub, LANES))
    return out.reshape(cap, d)


def _moe(h, logits, W, layer):
    n, d = h.shape
    top_val, top_idx = lax.top_k(logits, TOP_K)
    gates = jax.nn.softmax(top_val, axis=-1)
    flat_e = top_idx.reshape(-1)
    order = jnp.argsort(flat_e)
    sorted_e = flat_e[order]
    tok = (order // TOP_K).astype(jnp.int32)
    w_sorted = gates.reshape(-1)[order]
    counts = jnp.bincount(flat_e, length=N_EXPERTS)
    padded = (counts + MOE_ROWS - 1) // MOE_ROWS * MOE_ROWS
    pad_end = jnp.cumsum(padded)
    pad_start = pad_end - padded
    grp_start = jnp.cumsum(counts) - counts
    rank = jnp.arange(n * TOP_K) - grp_start[sorted_e]
    dest = pad_start[sorted_e] + rank
    n_items = -(-(n * TOP_K) // MOE_ROWS) + N_EXPERTS
    cap = n_items * MOE_ROWS
    row_tok = jnp.zeros((cap,), jnp.int32).at[dest].set(tok)
    row_w = jnp.zeros((cap,), F32).at[dest].set(w_sorted)
    block_e = jnp.minimum(jnp.searchsorted(pad_end, jnp.arange(n_items) * MOE_ROWS, side='right'),
                          N_EXPERTS - 1).astype(jnp.int32)
    first = jnp.concatenate([jnp.ones((1,), jnp.int32),
                             (block_e[1:] != block_e[:-1]).astype(jnp.int32)])
    xs = _gather_rows(h, row_tok)
    ys = _moe_ffn(xs, row_w, block_e, first, W['w_gu'], W['b_gu'], W['w_dn'], W['b_dn'], layer)
    return jnp.zeros((n, d), F32).at[row_tok].add(ys)


def _mixer_tail(proj, a, W, layer):
    hy = _hyena_branch(proj, W, layer)
    cvraw = _dwconv(proj, OFF_CV, CV_WIDTH, W['cv_dw_w'], W['cv_dw_b'], layer, glu=True)
    return _merge1(proj, a, hy, cvraw, W, layer)


def _trunk_layer(x, ctx, c8, W, ng3, layer, need_ctx):
    s = x.shape[0]
    lc = ctx.shape[0]
    lam_init = 0.8 - 0.6 * math.exp(-0.3 * layer)
    mod3 = _ada(c8, W['w_ada'], W['b_ada'], layer).reshape(-1, 1, D_MODEL)
    h_lat = _norm_mod(x, ng3, mod3, layer * 4, 1, 0)
    h_ctx = _norm_mod(ctx, ng3, mod3, layer * 4, 6 + 1, 6 + 0)
    attn = functools.partial(_diff_attention, da_lambda=W['da_lambda'], da_subln_g=W['da_subln_g'],
                             layer=layer, lam_init=lam_init)
    proj = _mm(h_lat, W['w_in'], layer)
    tabs = _rope_tables(s)
    if need_ctx:
        proj_c = _mm(h_ctx, W['w_in'], layer)
        kt_all, v_all = _kv_prep(proj, proj_c, OFF_K // K_COLS, tabs)
    else:
        kv_c = _mm(h_ctx, W['w_in'], layer, col_off=OFF_K, n_cols=K_COLS + V_COLS)
        kt_all, v_all = _kv_prep(proj, kv_c, 0, tabs)
    a_lat = attn(_q_prep(proj, tabs, rope=True), kt_all, v_all)
    x, f_lat, lg_lat = _merge2(_mixer_tail(proj, a_lat, W, layer), x, W, ng3, mod3, layer, 0)
    if need_ctx:
        a_ctx = attn(_q_prep(proj_c, tabs, rope=False), kt_all[:, :lc], v_all[:lc])
        ctx, f_ctx, lg_ctx = _merge2(_mixer_tail(proj_c, a_ctx, W, layer), ctx, W, ng3, mod3, layer, 1)
        f = _moe(jnp.concatenate([f_ctx, f_lat], axis=0), jnp.concatenate([lg_ctx, lg_lat], axis=0), W, layer)
        ctx = _resid_norm(ctx, f, ng3, mod3, layer * 4 + 3, 6 + 5)
        x = _resid_norm(x, f, ng3, mod3, layer * 4 + 3, 5, row_off=lc)
    else:
        f = _moe(f_lat, lg_lat, W, layer)
        x = _resid_norm(x, f, ng3, mod3, layer * 4 + 3, 5)
    return x, ctx


def kernel(x, c, ctx, c_ctx, w_ada, b_ada, norm_g, w_in, b_gate, da_lambda, da_subln_g, w_da_out, hy_short_w, hy_short_b, hy_w1, hy_b1, hy_w2, hy_b2, hy_freq, hy_w3, hy_bias, w_hy_out, cv_dw_w, cv_dw_b, cv_ln_g, cv_ln_b, w_cv_out, w_out, w_router, b_router, w_gu, b_gu, w_dn, b_dn):
    W = dict(w_ada=w_ada, b_ada=b_ada, w_in=w_in, b_gate=b_gate, da_lambda=da_lambda, da_subln_g=da_subln_g,
             w_da_out=w_da_out, hy_short_w=hy_short_w, hy_short_b=hy_short_b, hy_w1=hy_w1, hy_b1=hy_b1,
             hy_w2=hy_w2, hy_b2=hy_b2, hy_freq=hy_freq, hy_w3=hy_w3, hy_bias=hy_bias, w_hy_out=w_hy_out,
             cv_dw_w=cv_dw_w, cv_dw_b=cv_dw_b, cv_ln_g=cv_ln_g, cv_ln_b=cv_ln_b, w_cv_out=w_cv_out,
             w_out=w_out, w_router=w_router, b_router=b_router, w_gu=w_gu, b_gu=b_gu, w_dn=w_dn, b_dn=b_dn)
    assert x.shape[0] == 1 and ctx.shape[0] == 1
    xl = x[0]
    cl = ctx[0]
    c8 = jnp.concatenate([c, c_ctx[None], jnp.zeros((6, D_MODEL), F32)], axis=0)
    ng3 = norm_g.reshape(DEPTH * 4, 1, D_MODEL)
    for layer in range(DEPTH):
        xl, cl = _trunk_layer(xl, cl, c8, W, ng3, layer, layer < DEPTH - 1)
    return xl[None]
```
